```python
import math
import jax, jax.numpy as jnp
from jax import lax
import numpy as np

D_MODEL = 1024
BATCH = 8
SEQ = 4096
DEPTH = 4

HEAD_DIM = 64
N_ATT_HEADS = (D_MODEL // 2) // HEAD_DIM
ATT_WIDTH = N_ATT_HEADS * HEAD_DIM
LRU_WIDTH = D_MODEL - ATT_WIDTH
LRU_BLOCKS = 8
LRU_BLOCK_W = LRU_WIDTH // LRU_BLOCKS
LRU_C = 8.0
CONV_W = 4
MIX_WIDTH = ATT_WIDTH + LRU_WIDTH
N_IN = 3 * ATT_WIDTH + N_ATT_HEADS + 2 * LRU_WIDTH
Q_BLOCK = 128
N_GROUPS = 4
EXPERTS_PER_GROUP = 4
N_EXPERTS = N_GROUPS * EXPERTS_PER_GROUP
TOP_K = 2
D_EXPERT = D_MODEL // 4
ALPHA = (2.0 * DEPTH) ** 0.25
BETA = (8.0 * DEPTH) ** -0.25
LN_EPS = 1e-5
RMS_EPS = 1e-6
NEG_INF = -1e30

kernel_name = "hybrid_fox_rglru_grouped_moe_deepnorm"


def layer_norm(x, g, b):
    xf = x.astype(jnp.float32)
    mu = jnp.mean(xf, axis=-1, keepdims=True)
    var = jnp.mean(jnp.square(xf - mu), axis=-1, keepdims=True)
    y = (xf - mu) * lax.rsqrt(var + LN_EPS)
    return (y * g.astype(jnp.float32) + b.astype(jnp.float32)).astype(x.dtype)


def group_rms_norm(y, g):
    B, S, _ = y.shape
    yf = y.astype(jnp.float32)
    ya = yf[..., :ATT_WIDTH]
    yr = yf[..., ATT_WIDTH:]
    ya = ya * lax.rsqrt(jnp.mean(ya * ya, axis=-1, keepdims=True) + RMS_EPS)
    yr = yr * lax.rsqrt(jnp.mean(yr * yr, axis=-1, keepdims=True) + RMS_EPS)
    out = jnp.concatenate([ya, yr], axis=-1) * g.astype(jnp.float32)
    return out.astype(y.dtype)


def forgetting_attention(q, k, v, forget_logit):
    B, S, H, dh = q.shape
    scale = 1.0 / math.sqrt(dh)
    c = jnp.cumsum(jax.nn.log_sigmoid(forget_logit.astype(jnp.float32)), axis=1)
    c_keys = jnp.transpose(c, (0, 2, 1))
    nb = S // Q_BLOCK
    q_blocks = jnp.transpose(q.reshape(B, nb, Q_BLOCK, H, dh), (1, 0, 2, 3, 4))
    c_blocks = jnp.transpose(c.reshape(B, nb, Q_BLOCK, H), (1, 0, 3, 2))
    starts = jnp.arange(nb, dtype=jnp.int32) * Q_BLOCK
    key_pos = jnp.arange(S, dtype=jnp.int32)

    def one_block(args):
        q_blk, c_blk, start = args
        s = jnp.einsum('bqhd,bkhd->bhqk', q_blk, k).astype(jnp.float32) * scale
        s = s + c_blk[..., :, None] - c_keys[:, :, None, :]
        q_pos = start + jnp.arange(Q_BLOCK, dtype=jnp.int32)
        causal = key_pos[None, :] <= q_pos[:, None]
        s = jnp.where(causal[None, None], s, NEG_INF)
        p = jax.nn.softmax(s, axis=-1)
        return jnp.einsum('bhqk,bkhd->bqhd', p.astype(v.dtype), v)

    out = lax.map(one_block, (q_blocks, c_blocks, starts))
    return jnp.transpose(out, (1, 0, 2, 3, 4)).reshape(B, S, H * dh)


def causal_depthwise_conv(x, w, b):
    C = x.shape[-1]
    y = lax.conv_general_dilated(x, w[:, None, :].astype(x.dtype), window_strides=(1,),
                                 padding=[(CONV_W - 1, 0)],
                                 dimension_numbers=('NWC', 'WIO', 'NWC'),
                                 feature_group_count=C)
    return y + b


def rg_lru(xr, w_rec_gate, b_rec_gate, w_in_gate, b_in_gate, lru_lambda):
    B, S, C = xr.shape
    xf = xr.astype(jnp.float32)
    xh = xf.reshape(B, S, LRU_BLOCKS, LRU_BLOCK_W)
    r = jax.nn.sigmoid(jnp.einsum('bsnc,ncd->bsnd', xh, w_rec_gate.astype(jnp.float32)).reshape(B, S, C)
                       + b_rec_gate.astype(jnp.float32))
    i = jax.nn.sigmoid(jnp.einsum('bsnc,ncd->bsnd', xh, w_in_gate.astype(jnp.float32)).reshape(B, S, C)
                       + b_in_gate.astype(jnp.float32))
    log_a = -LRU_C * r * jax.nn.softplus(-lru_lambda.astype(jnp.float32))
    a = jnp.exp(log_a)
    u = jnp.sqrt(jnp.maximum(-jnp.expm1(2.0 * log_a), 0.0)) * (i * xf)

    def combine(earlier, later):
        a1, b1 = earlier
        a2, b2 = later
        return a1 * a2, a2 * b1 + b2

    _, h = lax.associative_scan(combine, (a, u), axis=1)
    return h.astype(xr.dtype)


def shared_router_combine(x, w_router, b_router):
    logits = (jnp.einsum('bsd,de->bse', x, w_router) + b_router).astype(jnp.float32)
    probs = jax.nn.softmax(logits, axis=-1)
    grouped = probs.reshape(probs.shape[:-1] + (N_GROUPS, EXPERTS_PER_GROUP))
    top_in_group, _ = lax.top_k(grouped, TOP_K)
    group_sel = jnp.argmax(jnp.sum(top_in_group, axis=-1), axis=-1)
    in_group = jnp.take_along_axis(grouped, group_sel[..., None, None], axis=-2)[..., 0, :]
    vals, idx = lax.top_k(in_group, TOP_K)
    gates = vals / jnp.sum(vals, axis=-1, keepdims=True)
    expert_idx = group_sel[..., None] * EXPERTS_PER_GROUP + idx
    comb = jnp.sum(gates[..., None] * jax.nn.one_hot(expert_idx, N_EXPERTS, dtype=jnp.float32), axis=-2)
    return comb.astype(x.dtype)


def moe_ffn(x, comb, w_gate, w_up, w_down):
    g = jnp.einsum('bsd,edf->bsef', x, w_gate)
    u = jnp.einsum('bsd,edf->bsef', x, w_up)
    act = jax.nn.silu(g) * u * comb[..., None]
    return jnp.einsum('bsef,efd->bsd', act, w_down)


def setup_inputs(seed: int = 0) -> dict:
    key = jax.random.key(seed)
    ks = jax.random.split(key, 24)
    f32 = jnp.float32
    nrm = lambda k, shape, s: jax.random.normal(k, shape, f32) * s
    u = jax.random.uniform(ks[9], (DEPTH, LRU_WIDTH), f32, 0.9, 0.999)
    a0 = u ** (1.0 / LRU_C)
    return {
        "x": nrm(ks[0], (BATCH, SEQ, D_MODEL), 1.0),
        "w_in": nrm(ks[1], (DEPTH, D_MODEL, N_IN), D_MODEL ** -0.5),
        "b_forget": 3.0 + nrm(ks[2], (DEPTH, N_ATT_HEADS), 0.5),
        "conv_w": nrm(ks[3], (DEPTH, CONV_W, LRU_WIDTH), CONV_W ** -0.5),
        "conv_b": nrm(ks[4], (DEPTH, LRU_WIDTH), 0.02),
        "w_rec_gate": nrm(ks[5], (DEPTH, LRU_BLOCKS, LRU_BLOCK_W, LRU_BLOCK_W), LRU_BLOCK_W ** -0.5),
        "b_rec_gate": nrm(ks[6], (DEPTH, LRU_WIDTH), 0.02),
        "w_in_gate": nrm(ks[7], (DEPTH, LRU_BLOCKS, LRU_BLOCK_W, LRU_BLOCK_W), LRU_BLOCK_W ** -0.5),
        "b_in_gate": nrm(ks[8], (DEPTH, LRU_WIDTH), 0.02),
        "lru_lambda": jnp.log(a0 / (1.0 - a0)),
        "mix_norm_g": 1.0 + nrm(ks[10], (DEPTH, MIX_WIDTH), 0.02),
        "w_out": nrm(ks[11], (DEPTH, MIX_WIDTH, D_MODEL), BETA * MIX_WIDTH ** -0.5),
        "ln1_g": 1.0 + nrm(ks[12], (DEPTH, D_MODEL), 0.02),
        "ln1_b": nrm(ks[13], (DEPTH, D_MODEL), 0.02),
        "w_router": nrm(ks[14], (D_MODEL, N_EXPERTS), D_MODEL ** -0.5),
        "b_router": nrm(ks[15], (N_EXPERTS,), 0.01),
        "w_exp_gate": nrm(ks[16], (DEPTH, N_EXPERTS, D_MODEL, D_EXPERT), D_MODEL ** -0.5),
        "w_exp_up": nrm(ks[17], (DEPTH, N_EXPERTS, D_MODEL, D_EXPERT), D_MODEL ** -0.5),
        "w_exp_down": nrm(ks[18], (DEPTH, N_EXPERTS, D_EXPERT, D_MODEL), BETA * D_EXPERT ** -0.5),
        "ln2_g": 1.0 + nrm(ks[19], (DEPTH, D_MODEL), 0.02),
        "ln2_b": nrm(ks[20], (DEPTH, D_MODEL), 0.02),
    }


def reference(x, w_in, b_forget, conv_w, conv_b, w_rec_gate, b_rec_gate, w_in_gate, b_in_gate,
              lru_lambda, mix_norm_g, w_out, ln1_g, ln1_b, w_router, b_router,
              w_exp_gate, w_exp_up, w_exp_down, ln2_g, ln2_b):
    B, S, _ = x.shape
    o_q, o_k, o_v = 0, ATT_WIDTH, 2 * ATT_WIDTH
    o_f = 3 * ATT_WIDTH
    o_x = o_f + N_ATT_HEADS
    o_g = o_x + LRU_WIDTH
    for l in range(DEPTH):
        proj = jnp.einsum('bsd,dn->bsn', x, w_in[l])
        q = proj[..., o_q:o_k].reshape(B, S, N_ATT_HEADS, HEAD_DIM)
        k = proj[..., o_k:o_v].reshape(B, S, N_ATT_HEADS, HEAD_DIM)
        v = proj[..., o_v:o_f].reshape(B, S, N_ATT_HEADS, HEAD_DIM)
        f_logit = proj[..., o_f:o_x] + b_forget[l]
        attn = forgetting_attention(q, k, v, f_logit)
        xr = causal_depthwise_conv(proj[..., o_x:o_g], conv_w[l], conv_b[l])
        h = rg_lru(xr, w_rec_gate[l], b_rec_gate[l], w_in_gate[l], b_in_gate[l], lru_lambda[l])
        rec = jax.nn.gelu(proj[..., o_g:]) * h
        mixed = group_rms_norm(jnp.concatenate([attn, rec], axis=-1), mix_norm_g[l])
        x = layer_norm(ALPHA * x + jnp.einsum('bsm,md->bsd', mixed, w_out[l]), ln1_g[l], ln1_b[l])
        comb = shared_router_combine(x, w_router, b_router)
        x = layer_norm(ALPHA * x + moe_ffn(x, comb, w_exp_gate[l], w_exp_up[l], w_exp_down[l]),
                       ln2_g[l], ln2_b[l])
    return x
```

```python
import functools

import jax
import jax.numpy as jnp
from jax import lax
from jax.experimental import pallas as pl
from jax.experimental.pallas import tpu as pltpu

F32 = jnp.float32
BF16 = jnp.bfloat16

HEAD_DIM = 64
N_HEADS = 8
ATT_WIDTH = N_HEADS * HEAD_DIM
LRU_WIDTH = 512
LRU_C = 8.0
CONV_W = 4
N_GROUPS = 4
EXPERTS_PER_GROUP = 4
N_EXPERTS = N_GROUPS * EXPERTS_PER_GROUP
LN_EPS = 1e-5
RMS_EPS = 1e-6
NEG_INF = -1e30

LANES = 128
SUBLANES = 8
HEADS_PER_BLOCK = LANES // HEAD_DIM
N_HEAD_BLOCKS = N_HEADS // HEADS_PER_BLOCK
BIAS_SLOT = 8
VMEM_LIMIT = 48 * 1024 * 1024


def _cparams(sem):
    return pltpu.CompilerParams(dimension_semantics=sem, vmem_limit_bytes=VMEM_LIMIT)


def _in_proj_kernel(x_ref, wqkv_ref, wf_ref, bf_ref, wxg_ref, qq_ref, kk_ref, v_ref, xg_ref, carry_ref):
    tm = x_ref.shape[1]

    @pl.when(pl.program_id(1) == 0)
    def _():
        carry_ref[...] = jnp.zeros_like(carry_ref)

    xb = x_ref[0].astype(BF16)
    qkv = jnp.dot(xb, wqkv_ref[...], preferred_element_type=F32)
    xg_ref[0] = jnp.dot(xb, wxg_ref[...], preferred_element_type=F32)
    f = jnp.dot(xb, wf_ref[...], preferred_element_type=F32) + bf_ref[...]

    c = jnp.minimum(f, 0.0) - jnp.log1p(jnp.exp(-jnp.abs(f)))
    row = lax.broadcasted_iota(jnp.int32, c.shape, 0)
    d = 1
    while d < tm:
        c = c + jnp.where(row >= d, pltpu.roll(c, d, axis=0), 0.0)
        d *= 2
    c = c + carry_ref[...]
    carry_ref[...] = c[tm - 1:tm, :]

    parts = []
    for h in range(N_HEADS):
        col = c[:, h:h + 1]
        hi = col.astype(BF16).astype(F32)
        rem = col - hi
        mid = rem.astype(BF16).astype(F32)
        parts.append((hi, mid, rem - mid))

    lane = lax.broadcasted_iota(jnp.int32, (tm, LANES), 1)
    slot = lane % BIAS_SLOT
    head_slot = lane // BIAS_SLOT
    valid = head_slot < HEADS_PER_BLOCK
    for hb in range(N_HEAD_BLOCKS):
        def part(k):
            return jnp.where(head_slot == 0, parts[2 * hb][k], parts[2 * hb + 1][k])

        cq = jnp.where(slot == 0, part(0), jnp.where(slot == 1, part(1), part(2)))
        ck = jnp.where(slot == 3, part(0), jnp.where(slot == 4, part(1), part(2)))
        qx = jnp.where(valid & (slot < 3), cq, jnp.where(valid & (slot < 6), 1.0, 0.0))
        kx = jnp.where(valid & (slot < 3), 1.0, jnp.where(valid & (slot < 6), -ck, 0.0))
        q2 = qkv[:, hb * LANES:(hb + 1) * LANES] * (HEAD_DIM ** -0.5)
        k2 = qkv[:, ATT_WIDTH + hb * LANES:ATT_WIDTH + (hb + 1) * LANES]
        qq_ref[0, :, 2 * hb * LANES:(2 * hb + 1) * LANES] = q2.astype(BF16)
        qq_ref[0, :, (2 * hb + 1) * LANES:(2 * hb + 2) * LANES] = qx.astype(BF16)
        kk_ref[0, :, 2 * hb * LANES:(2 * hb + 1) * LANES] = k2.astype(BF16)
        kk_ref[0, :, (2 * hb + 1) * LANES:(2 * hb + 2) * LANES] = kx.astype(BF16)
    v_ref[0] = qkv[:, 2 * ATT_WIDTH:].astype(BF16)


def _in_proj(x, wqkv, wf, bf, wxg, tm):
    B, S, D = x.shape
    const = lambda b, s: (0, 0)
    tile = lambda b, s: (b, s, 0)
    return pl.pallas_call(
        _in_proj_kernel,
        grid=(B, S // tm),
        in_specs=[
            pl.BlockSpec((1, tm, D), tile),
            pl.BlockSpec(wqkv.shape, const),
            pl.BlockSpec(wf.shape, const),
            pl.BlockSpec(bf.shape, const),
            pl.BlockSpec(wxg.shape, const),
        ],
        out_specs=[
            pl.BlockSpec((1, tm, 2 * ATT_WIDTH), tile),
            pl.BlockSpec((1, tm, 2 * ATT_WIDTH), tile),
            pl.BlockSpec((1, tm, ATT_WIDTH), tile),
            pl.BlockSpec((1, tm, 2 * LRU_WIDTH), tile),
        ],
        out_shape=[
            jax.ShapeDtypeStruct((B, S, 2 * ATT_WIDTH), BF16),
            jax.ShapeDtypeStruct((B, S, 2 * ATT_WIDTH), BF16),
            jax.ShapeDtypeStruct((B, S, ATT_WIDTH), BF16),
            jax.ShapeDtypeStruct((B, S, 2 * LRU_WIDTH), F32),
        ],
        scratch_shapes=[pltpu.VMEM((1, LANES), F32)],
        compiler_params=_cparams(("parallel", "arbitrary")),
        name="in_proj",
    )(x, wqkv, wf, bf, wxg)


def _attn_kernel(qq_ref, kk_ref, v_ref, o_ref, m_ref, l_ref, acc_ref, *, blk):
    qi = pl.program_id(2)
    q = qq_ref[0]
    lane = lax.broadcasted_iota(jnp.int32, q.shape, 1)
    zero = jnp.zeros_like(q)
    q_heads = []
    for h in range(HEADS_PER_BLOCK):
        own = ((lane >= h * HEAD_DIM) & (lane < (h + 1) * HEAD_DIM)) | (
            (lane >= LANES + h * BIAS_SLOT) & (lane < LANES + (h + 1) * BIAS_SLOT))
        q_heads.append(jnp.where(own, q, zero))

    m_ref[...] = jnp.full_like(m_ref, NEG_INF)
    l_ref[...] = jnp.zeros_like(l_ref)
    acc_ref[...] = jnp.zeros_like(acc_ref)

    def step(off, mask):
        k_blk = kk_ref[0, pl.ds(off, blk), :]
        v_blk = v_ref[0, pl.ds(off, blk), :]
        for h in range(HEADS_PER_BLOCK):
            s = lax.dot_general(q_heads[h], k_blk, (((1,), (1,)), ((), ())),
                                preferred_element_type=F32)
            if mask is not None:
                s = jnp.where(mask, s, NEG_INF)
            m_old = m_ref[h]
            m_new = jnp.maximum(m_old, jnp.max(s, axis=1, keepdims=True))
            alpha = jnp.exp(m_old - m_new)
            p = jnp.exp(s - m_new)
            l_ref[h] = alpha * l_ref[h] + jnp.sum(p, axis=1, keepdims=True)
            acc_ref[h] = alpha * acc_ref[h] + jnp.dot(p.astype(BF16), v_blk, preferred_element_type=F32)
            m_ref[h] = m_new

    def body(ki, carry):
        step(pl.multiple_of(ki * blk, blk), None)
        return carry

    lax.fori_loop(0, qi, body, 0)
    row = lax.broadcasted_iota(jnp.int32, (blk, blk), 0)
    col = lax.broadcasted_iota(jnp.int32, (blk, blk), 1)
    step(pl.multiple_of(qi * blk, blk), col <= row)

    out_lane = lax.broadcasted_iota(jnp.int32, (blk, LANES), 1)
    o = jnp.where(out_lane < HEAD_DIM, acc_ref[0] / l_ref[0], acc_ref[1] / l_ref[1])
    o_ref[0] = o.astype(o_ref.dtype)


def _attention(qq, kk, v, blk):
    B, S, _ = v.shape
    return pl.pallas_call(
        functools.partial(_attn_kernel, blk=blk),
        grid=(B, N_HEAD_BLOCKS, S // blk),
        in_specs=[
            pl.BlockSpec((1, blk, 2 * LANES), lambda b, h, i: (b, i, h)),
            pl.BlockSpec((1, S, 2 * LANES), lambda b, h, i: (b, 0, h)),
            pl.BlockSpec((1, S, LANES), lambda b, h, i: (b, 0, h)),
        ],
        out_specs=pl.BlockSpec((1, blk, LANES), lambda b, h, i: (b, i, h)),
        out_shape=jax.ShapeDtypeStruct((B, S, ATT_WIDTH), BF16),
        scratch_shapes=[
            pltpu.VMEM((HEADS_PER_BLOCK, blk, 1), F32),
            pltpu.VMEM((HEADS_PER_BLOCK, blk, 1), F32),
            pltpu.VMEM((HEADS_PER_BLOCK, blk, LANES), F32),
        ],
        compiler_params=_cparams(("parallel", "parallel", "arbitrary")),
        name="fox_attention",
    )(qq, kk, v)


def _lru_kernel(x_ref, g_ref, cw_ref, cb_ref, wri_ref, bri_ref, lam_ref, o_ref,
                ext_ref, a_ref, u_ref, h_ref):
    ts = x_ref.shape[1]
    W = x_ref.shape[2]

    @pl.when(pl.program_id(1) == 0)
    def _():
        ext_ref[0:SUBLANES, :] = jnp.zeros((SUBLANES, W), F32)
        h_ref[...] = jnp.zeros_like(h_ref)

    x = x_ref[0]
    ext_ref[SUBLANES:SUBLANES + ts, :] = x
    xc = cb_ref[...] + cw_ref[CONV_W - 1:CONV_W, :] * x
    for back in range(1, CONV_W):
        xc = xc + cw_ref[CONV_W - 1 - back:CONV_W - back, :] * ext_ref[SUBLANES - back:SUBLANES - back + ts, :]
    ext_ref[0:SUBLANES, :] = x[ts - SUBLANES:ts, :]

    ri = jnp.dot(xc.astype(BF16), wri_ref[...], preferred_element_type=F32) + bri_ref[...]
    r = jax.nn.sigmoid(ri[:, :W])
    i = jax.nn.sigmoid(ri[:, W:])
    neg_lam = -lam_ref[...]
    softplus = jnp.maximum(neg_lam, 0.0) + jnp.log1p(jnp.exp(-jnp.abs(neg_lam)))
    a = jnp.exp(-LRU_C * r * softplus)
    a_ref[...] = a
    u_ref[...] = jnp.sqrt(jnp.maximum(1.0 - a * a, 0.0)) * (i * xc)

    row = lax.broadcasted_iota(jnp.int32, (SUBLANES, W), 0)

    def group(j, h_prev):
        off = pl.multiple_of(j * SUBLANES, SUBLANES)
        A = a_ref[pl.ds(off, SUBLANES), :]
        U = u_ref[pl.ds(off, SUBLANES), :]
        for d in (1, 2, 4):
            keep = row >= d
            U = U + A * jnp.where(keep, pltpu.roll(U, d, axis=0), 0.0)
            A = A * jnp.where(keep, pltpu.roll(A, d, axis=0), 1.0)
        H = A * h_prev + U
        u_ref[pl.ds(off, SUBLANES), :] = H
        return jnp.broadcast_to(H[SUBLANES - 1:SUBLANES, :], (SUBLANES, W))

    h_ref[...] = lax.fori_loop(0, ts // SUBLANES, group, h_ref[...])

    gate = g_ref[0]
    gelu = 0.5 * gate * (1.0 + jnp.tanh(0.7978845608028654 * (gate + 0.044715 * (gate * gate * gate))))
    o_ref[0] = (gelu * u_ref[...]).astype(o_ref.dtype)


def _lru(xg, cw, cb, wri, bri, lam, ts):
    B, S, _ = xg.shape
    W = LRU_WIDTH
    const = lambda b, s: (0, 0)
    return pl.pallas_call(
        _lru_kernel,
        grid=(B, S // ts),
        in_specs=[
            pl.BlockSpec((1, ts, W), lambda b, s: (b, s, 0)),
            pl.BlockSpec((1, ts, W), lambda b, s: (b, s, 1)),
            pl.BlockSpec(cw.shape, const),
            pl.BlockSpec(cb.shape, const),
            pl.BlockSpec(wri.shape, const),
            pl.BlockSpec(bri.shape, const),
            pl.BlockSpec(lam.shape, const),
        ],
        out_specs=pl.BlockSpec((1, ts, W), lambda b, s: (b, s, 0)),
        out_shape=jax.ShapeDtypeStruct((B, S, W), BF16),
        scratch_shapes=[
            pltpu.VMEM((ts + SUBLANES, W), F32),
            pltpu.VMEM((ts, W), F32),
            pltpu.VMEM((ts, W), F32),
            pltpu.VMEM((SUBLANES, W), F32),
        ],
        compiler_params=_cparams(("parallel", "arbitrary")),
        name="rg_lru",
    )(xg, xg, cw, cb, wri, bri, lam)


def _layer_norm(y, g, b):
    mu = jnp.mean(y, axis=-1, keepdims=True)
    yc = y - mu
    var = jnp.mean(yc * yc, axis=-1, keepdims=True)
    return yc * lax.rsqrt(var + LN_EPS) * g + b


def _route(logits_t):
    mx = jnp.max(logits_t, axis=0, keepdims=True)
    e = jnp.exp(logits_t - mx)
    p = e / jnp.sum(e, axis=0, keepdims=True)
    v = [p[SUBLANES * j:SUBLANES * (j + 1), :] for j in range(EXPERTS_PER_GROUP)]

    def first_max(vals):
        best = functools.reduce(jnp.maximum, vals)
        idx = jnp.full(best.shape, len(vals) - 1, jnp.int32)
        for j in range(len(vals) - 2, -1, -1):
            idx = jnp.where(vals[j] == best, j, idx)
        return best, idx

    m1, idx1 = first_max(v)
    m2, idx2 = first_max([jnp.where(idx1 == j, -1.0, v[j]) for j in range(EXPERTS_PER_GROUP)])
    grp = lax.broadcasted_iota(jnp.int32, m1.shape, 0)
    real = grp < N_GROUPS
    score = jnp.where(real, m1 + m2, -1.0)
    best = jnp.max(score, axis=0, keepdims=True)
    chosen = jnp.min(jnp.where(score == best, grp, SUBLANES), axis=0, keepdims=True)
    sel = grp == chosen
    den = jnp.where(real, m1 + m2, 1.0)
    g1 = m1 / den
    g2 = m2 / den
    return [jnp.where(sel, jnp.where(idx1 == j, g1, 0.0) + jnp.where(idx2 == j, g2, 0.0), 0.0)
            for j in range(EXPERTS_PER_GROUP)]


def _mix_kernel(x_ref, attn_ref, rec_ref, ng_ref, wout_ref, lng_ref, lnb_ref, wr_ref, br_ref,
                x1_ref, x1b_ref, comb_ref, *, alpha):
    tm = x_ref.shape[0]
    ya = attn_ref[...].astype(F32)
    yr = rec_ref[...].astype(F32)
    ya = ya * lax.rsqrt(jnp.mean(ya * ya, axis=-1, keepdims=True) + RMS_EPS) * ng_ref[:, :ATT_WIDTH]
    yr = yr * lax.rsqrt(jnp.mean(yr * yr, axis=-1, keepdims=True) + RMS_EPS) * ng_ref[:, ATT_WIDTH:]
    out = jnp.dot(ya.astype(BF16), wout_ref[:ATT_WIDTH, :], preferred_element_type=F32)
    out = out + jnp.dot(yr.astype(BF16), wout_ref[ATT_WIDTH:, :], preferred_element_type=F32)
    x1 = _layer_norm(alpha * x_ref[...] + out, lng_ref[...], lnb_ref[...])
    x1_ref[...] = x1
    x1_hi = x1.astype(BF16)
    x1b_ref[...] = x1_hi

    x1_lo = (x1 - x1_hi.astype(F32)).astype(BF16)
    nt = (((1,), (1,)), ((), ()))
    o = (lax.dot_general(wr_ref[...], x1_hi, nt, preferred_element_type=F32)
         + lax.dot_general(wr_ref[...], x1_lo, nt, preferred_element_type=F32))
    rows = 4 * SUBLANES
    logits_t = o[:rows, :] + o[rows:, :] + br_ref[...]
    comb_t = _route(logits_t)
    comb_t = jnp.concatenate(comb_t + [jnp.zeros((LANES - rows, tm), F32)], axis=0)
    comb_ref[...] = comb_t.T


def _mix_out(x2d, attn2d, rec2d, ng, wout, lng, lnb, wr, br, alpha, tm):
    T, D = x2d.shape
    const = lambda i: (0, 0)
    tile = lambda i: (i, 0)
    return pl.pallas_call(
        functools.partial(_mix_kernel, alpha=alpha),
        grid=(T // tm,),
        in_specs=[
            pl.BlockSpec((tm, D), tile),
            pl.BlockSpec((tm, ATT_WIDTH), tile),
            pl.BlockSpec((tm, LRU_WIDTH), tile),
            pl.BlockSpec(ng.shape, const),
            pl.BlockSpec(wout.shape, const),
            pl.BlockSpec(lng.shape, const),
            pl.BlockSpec(lnb.shape, const),
            pl.BlockSpec(wr.shape, const),
            pl.BlockSpec(br.shape, const),
        ],
        out_specs=[
            pl.BlockSpec((tm, D), tile),
            pl.BlockSpec((tm, D), tile),
            pl.BlockSpec((tm, LANES), tile),
        ],
        out_shape=[
            jax.ShapeDtypeStruct((T, D), F32),
            jax.ShapeDtypeStruct((T, D), BF16),
            jax.ShapeDtypeStruct((T, LANES), F32),
        ],
        compiler_params=_cparams(("parallel",)),
        name="mix_out_router",
    )(x2d, attn2d, rec2d, ng, wout, lng, lnb, wr, br)


def _moe_kernel(x1_ref, xb_ref, comb_ref, wgu_ref, wd_ref, lng_ref, lnb_ref, o_ref, *, alpha):
    g = pl.program_id(1)
    F = wd_ref.shape[1]

    @pl.when(g == 0)
    def _():
        o_ref[...] = jnp.zeros_like(o_ref)

    xb = xb_ref[...]
    comb = comb_ref[...]
    lane = lax.broadcasted_iota(jnp.int32, comb.shape, 1)
    acc = o_ref[...]
    for j in range(EXPERTS_PER_GROUP):
        gu = jnp.dot(xb, wgu_ref[j], preferred_element_type=F32)
        gate = gu[:, :F]
        w = jnp.sum(jnp.where(lane == SUBLANES * j + g, comb, 0.0), axis=1, keepdims=True)
        act = gate * jax.nn.sigmoid(gate) * gu[:, F:] * w
        acc = acc + jnp.dot(act.astype(BF16), wd_ref[j], preferred_element_type=F32)
    o_ref[...] = acc

    @pl.when(g == N_GROUPS - 1)
    def _():
        o_ref[...] = _layer_norm(alpha * x1_ref[...] + o_ref[...], lng_ref[...], lnb_ref[...])


def _moe(x1, x1b, comb, wgu, wd, lng, lnb, alpha, tm):
    T, D = x1.shape
    F = wd.shape[1]
    const = lambda i, g: (0, 0)
    tile = lambda i, g: (i, 0)
    return pl.pallas_call(
        functools.partial(_moe_kernel, alpha=alpha),
        grid=(T // tm, N_GROUPS),
        in_specs=[
            pl.BlockSpec((tm, D), tile),
            pl.BlockSpec((tm, D), tile),
            pl.BlockSpec((tm, LANES), tile),
            pl.BlockSpec((EXPERTS_PER_GROUP, D, 2 * F), lambda i, g: (g, 0, 0)),
            pl.BlockSpec((EXPERTS_PER_GROUP, F, D), lambda i, g: (g, 0, 0)),
            pl.BlockSpec(lng.shape, const),
            pl.BlockSpec(lnb.shape, const),
        ],
        out_specs=pl.BlockSpec((tm, D), tile),
        out_shape=jax.ShapeDtypeStruct((T, D), F32),
        compiler_params=_cparams(("parallel", "arbitrary")),
        name="moe_ffn",
    )(x1, x1b, comb, wgu, wd, lng, lnb)


def _block_diag(w):
    n, c, d = w.shape
    return jnp.einsum('ncd,nm->ncmd', w, jnp.eye(n, dtype=w.dtype)).reshape(n * c, n * d)


def _router_operands(w_router, b_router):
    D = w_router.shape[0]
    wt = w_router.T.reshape(N_GROUPS, EXPERTS_PER_GROUP, D).transpose(1, 0, 2)
    wt = jnp.pad(wt, ((0, 0), (0, SUBLANES - N_GROUPS), (0, 0))).reshape(4 * SUBLANES, D)
    hi = wt.astype(BF16)
    lo = (wt - hi.astype(F32)).astype(BF16)
    bt = b_router.reshape(N_GROUPS, EXPERTS_PER_GROUP).T
    bt = jnp.pad(bt, ((0, 0), (0, SUBLANES - N_GROUPS)), constant_values=NEG_INF).reshape(4 * SUBLANES, 1)
    return jnp.concatenate([hi, lo], axis=0), bt.astype(F32)


def kernel(x, w_in, b_forget, conv_w, conv_b, w_rec_gate, b_rec_gate, w_in_gate, b_in_gate, lru_lambda,
           mix_norm_g, w_out, ln1_g, ln1_b, w_router, b_router, w_exp_gate, w_exp_up, w_exp_down,
           ln2_g, ln2_b):
    B, S, D = x.shape
    depth = w_in.shape[0]
    alpha = (2.0 * depth) ** 0.25
    T = B * S
    o_f = 3 * ATT_WIDTH
    o_x = o_f + N_HEADS

    tm_proj = min(512, S)
    blk_attn = min(512, S)
    ts_lru = min(512, S)
    tm_mix = min(512, T)
    tm_moe = min(1024, T)

    wr, br = _router_operands(w_router, b_router)
    row = lambda a: a.reshape(1, -1).astype(F32)

    for l in range(depth):
        wqkv = w_in[l, :, :o_f].astype(BF16)
        wf = jnp.pad(w_in[l, :, o_f:o_x], ((0, 0), (0, LANES - N_HEADS))).astype(BF16)
        bf = jnp.pad(b_forget[l], (0, LANES - N_HEADS)).reshape(1, LANES).astype(F32)
        wxg = w_in[l, :, o_x:].astype(BF16)
        qq, kk, v, xg = _in_proj(x, wqkv, wf, bf, wxg, tm_proj)

        attn = _attention(qq, kk, v, blk_attn)

        wri = jnp.concatenate([_block_diag(w_rec_gate[l]), _block_diag(w_in_gate[l])], axis=1).astype(BF16)
        bri = jnp.concatenate([b_rec_gate[l], b_in_gate[l]]).reshape(1, -1).astype(F32)
        rec = _lru(xg, conv_w[l].astype(F32), row(conv_b[l]), wri, bri, row(lru_lambda[l]), ts_lru)

        x1, x1b, comb = _mix_out(
            x.reshape(T, D), attn.reshape(T, ATT_WIDTH), rec.reshape(T, LRU_WIDTH), row(mix_norm_g[l]),
            w_out[l].astype(BF16), row(ln1_g[l]), row(ln1_b[l]), wr, br, alpha, tm_mix)

        wgu = jnp.concatenate([w_exp_gate[l], w_exp_up[l]], axis=-1).astype(BF16)
        x = _moe(x1, x1b, comb, wgu, w_exp_down[l].astype(BF16), row(ln2_g[l]), row(ln2_b[l]),
                 alpha, tm_moe).reshape(B, S, D).astype(x.dtype)
    return x
```

```python
import functools

import jax
import jax.numpy as jnp
from jax import lax
from jax.experimental import pallas as pl
from jax.experimental.pallas import tpu as pltpu

F32 = jnp.float32
BF16 = jnp.bfloat16

HEAD_DIM = 64
N_HEADS = 8
ATT_WIDTH = N_HEADS * HEAD_DIM
LRU_WIDTH = 512
LRU_C = 8.0
CONV_W = 4
N_GROUPS = 4
EXPERTS_PER_GROUP = 4
N_EXPERTS = N_GROUPS * EXPERTS_PER_GROUP
LN_EPS = 1e-5
RMS_EPS = 1e-6
NEG_INF = -1e30

LANES = 128
SUBLANES = 8
HEADS_PER_BLOCK = LANES // HEAD_DIM
N_HEAD_BLOCKS = N_HEADS // HEADS_PER_BLOCK
BIAS_SLOT = 8
BF16_SUBLANES = 16
VT_ROWS = HEAD_DIM + BF16_SUBLANES
LOG2E = 1.4426950408889634
VMEM_LIMIT = 48 * 1024 * 1024


def _cparams(sem):
    return pltpu.CompilerParams(dimension_semantics=sem, vmem_limit_bytes=VMEM_LIMIT)


def _in_proj_kernel(x_ref, wqkv_ref, wf_ref, bf_ref, wxg_ref, qq_ref, kk_ref, vt_ref, xg_ref, carry_ref):
    tm = x_ref.shape[1]

    @pl.when(pl.program_id(1) == 0)
    def _():
        carry_ref[...] = jnp.zeros_like(carry_ref)

    xb = x_ref[0].astype(BF16)
    qkv = jnp.dot(xb, wqkv_ref[...], preferred_element_type=F32)
    xg_ref[0] = jnp.dot(xb, wxg_ref[...], preferred_element_type=F32)
    f = jnp.dot(xb, wf_ref[...], preferred_element_type=F32) + bf_ref[...]

    c = jnp.minimum(f, 0.0) - jnp.log1p(jnp.exp(-jnp.abs(f)))
    row = lax.broadcasted_iota(jnp.int32, c.shape, 0)
    d = 1
    while d < tm:
        c = c + jnp.where(row >= d, pltpu.roll(c, d, axis=0), 0.0)
        d *= 2
    c = c + carry_ref[...]
    carry_ref[...] = c[tm - 1:tm, :]

    c2 = c * LOG2E
    parts = []
    for h in range(N_HEADS):
        col = c2[:, h:h + 1]
        hi = col.astype(BF16).astype(F32)
        rem = col - hi
        mid = rem.astype(BF16).astype(F32)
        parts.append((hi, mid, rem - mid))

    lane = lax.broadcasted_iota(jnp.int32, (tm, LANES), 1)
    slot = lane % BIAS_SLOT
    head_slot = lane // BIAS_SLOT
    valid = head_slot < HEADS_PER_BLOCK
    for hb in range(N_HEAD_BLOCKS):
        def part(k):
            return jnp.where(head_slot == 0, parts[2 * hb][k], parts[2 * hb + 1][k])

        cq = jnp.where(slot == 0, part(0), jnp.where(slot == 1, part(1), part(2)))
        ck = jnp.where(slot == 3, part(0), jnp.where(slot == 4, part(1), part(2)))
        qx = jnp.where(valid & (slot < 3), cq, jnp.where(valid & (slot < 6), 1.0, 0.0))
        kx = jnp.where(valid & (slot < 3), 1.0, jnp.where(valid & (slot < 6), -ck, 0.0))
        q2 = qkv[:, hb * LANES:(hb + 1) * LANES] * (HEAD_DIM ** -0.5 * LOG2E)
        k2 = qkv[:, ATT_WIDTH + hb * LANES:ATT_WIDTH + (hb + 1) * LANES]
        qq_ref[0, :, 2 * hb * LANES:(2 * hb + 1) * LANES] = q2.astype(BF16)
        qq_ref[0, :, (2 * hb + 1) * LANES:(2 * hb + 2) * LANES] = qx.astype(BF16)
        kk_ref[0, :, 2 * hb * LANES:(2 * hb + 1) * LANES] = k2.astype(BF16)
        kk_ref[0, :, (2 * hb + 1) * LANES:(2 * hb + 2) * LANES] = kx.astype(BF16)
    vt = qkv[:, 2 * ATT_WIDTH:].T
    pad_row = lax.broadcasted_iota(jnp.int32, (VT_ROWS - HEAD_DIM, tm), 0)
    ones_then_zeros = jnp.where(pad_row == 0, 1.0, 0.0)
    pieces = []
    for h in range(N_HEADS):
        pieces += [vt[h * HEAD_DIM:(h + 1) * HEAD_DIM, :], ones_then_zeros]
    vt_ref[0, 0] = jnp.concatenate(pieces, axis=0).astype(BF16)


def _in_proj(x, wqkv, wf, bf, wxg, tm):
    B, S, D = x.shape
    const = lambda b, s: (0, 0)
    tile = lambda b, s: (b, s, 0)
    return pl.pallas_call(
        _in_proj_kernel,
        grid=(B, S // tm),
        in_specs=[
            pl.BlockSpec((1, tm, D), tile),
            pl.BlockSpec(wqkv.shape, const),
            pl.BlockSpec(wf.shape, const),
            pl.BlockSpec(bf.shape, const),
            pl.BlockSpec(wxg.shape, const),
        ],
        out_specs=[
            pl.BlockSpec((1, tm, 2 * ATT_WIDTH), tile),
            pl.BlockSpec((1, tm, 2 * ATT_WIDTH), tile),
            pl.BlockSpec((1, 1, N_HEADS * VT_ROWS, tm), lambda b, s: (b, s, 0, 0)),
            pl.BlockSpec((1, tm, 2 * LRU_WIDTH), tile),
        ],
        out_shape=[
            jax.ShapeDtypeStruct((B, S, 2 * ATT_WIDTH), BF16),
            jax.ShapeDtypeStruct((B, S, 2 * ATT_WIDTH), BF16),
            jax.ShapeDtypeStruct((B, S // tm, N_HEADS * VT_ROWS, tm), BF16),
            jax.ShapeDtypeStruct((B, S, 2 * LRU_WIDTH), F32),
        ],
        scratch_shapes=[pltpu.VMEM((1, LANES), F32)],
        compiler_params=_cparams(("parallel", "arbitrary")),
        name="in_proj",
    )(x, wqkv, wf, bf, wxg)


def _attn_kernel(qq_ref, kk_ref, vt_ref, o_ref, st0_ref, st1_ref, bm0_ref, bm1_ref, m_ref, acc_ref, *, blk):
    qi = pl.program_id(2)
    st_refs = (st0_ref, st1_ref)
    bm_refs = (bm0_ref, bm1_ref)
    q = qq_ref[0]
    lane = lax.broadcasted_iota(jnp.int32, q.shape, 1)
    zero = jnp.zeros_like(q)
    q_heads = []
    for h in range(HEADS_PER_BLOCK):
        own = ((lane >= h * HEAD_DIM) & (lane < (h + 1) * HEAD_DIM)) | (
            (lane >= LANES + h * BIAS_SLOT) & (lane < LANES + (h + 1) * BIAS_SLOT))
        q_heads.append(jnp.where(own, q, zero))

    m_ref[...] = jnp.full_like(m_ref, NEG_INF)
    acc_ref[...] = jnp.zeros_like(acc_ref)

    def scores(ki, h, slot, masked):
        k_blk = kk_ref[0, pl.ds(pl.multiple_of(ki * blk, blk), blk), :]
        st = lax.dot_general(k_blk, q_heads[h], (((1,), (1,)), ((), ())),
                             preferred_element_type=F32)
        if masked:
            key = lax.broadcasted_iota(jnp.int32, (blk, blk), 0)
            qry = lax.broadcasted_iota(jnp.int32, (blk, blk), 1)
            st = jnp.where(key - qry <= (qi - ki) * blk, st, NEG_INF)
        st_refs[slot][h] = st
        bm_refs[slot][h] = jnp.max(st, axis=0, keepdims=True)

    def update(ki, h, slot):
        m_old = m_ref[h]
        m_new = jnp.maximum(m_old, bm_refs[slot][h])
        alpha = jnp.exp2(m_old - m_new)
        p = jnp.exp2(st_refs[slot][h] - m_new).astype(BF16)
        vt = vt_ref[0, ki, h * VT_ROWS:(h + 1) * VT_ROWS, :]
        acc_ref[h] = alpha * acc_ref[h] + jnp.dot(vt, p, preferred_element_type=F32)
        m_ref[h] = m_new

    def stage(score_blk, score_slot, masked, update_blk, update_slot):
        for h in range(HEADS_PER_BLOCK):
            if score_blk is not None:
                scores(score_blk, h, score_slot, masked)
            if update_blk is not None:
                update(update_blk, h, update_slot)

    stage(0, 0, True, None, None)

    def body(j, carry):
        i = 2 * j
        stage(i + 1, 1, False, i, 0)
        stage(i + 2, 0, False, i + 1, 1)
        return carry

    lax.fori_loop(0, (qi - 1) // 2, body, 0)

    @pl.when(qi == 0)
    def _():
        stage(None, None, False, 0, 0)

    @pl.when(qi % 2 == 1)
    def _():
        stage(qi, 1, True, qi - 1, 0)
        stage(None, None, False, qi, 1)

    @pl.when((qi % 2 == 0) & (qi > 0))
    def _():
        stage(qi - 1, 1, False, qi - 2, 0)
        stage(qi, 0, True, qi - 1, 1)
        stage(None, None, False, qi, 0)

    ot = jnp.concatenate([acc_ref[h, :HEAD_DIM, :] / acc_ref[h, HEAD_DIM:HEAD_DIM + 1, :]
                          for h in range(HEADS_PER_BLOCK)], axis=0)
    o_ref[0] = ot.T.astype(o_ref.dtype)


def _attention(qq, kk, vt, blk):
    B, nblk, _, _ = vt.shape
    S = nblk * blk
    return pl.pallas_call(
        functools.partial(_attn_kernel, blk=blk),
        grid=(B, N_HEAD_BLOCKS, nblk),
        in_specs=[
            pl.BlockSpec((1, blk, 2 * LANES), lambda b, h, i: (b, i, h)),
            pl.BlockSpec((1, S, 2 * LANES), lambda b, h, i: (b, 0, h)),
            pl.BlockSpec((1, nblk, HEADS_PER_BLOCK * VT_ROWS, blk), lambda b, h, i: (b, 0, h, 0)),
        ],
        out_specs=pl.BlockSpec((1, blk, LANES), lambda b, h, i: (b, i, h)),
        out_shape=jax.ShapeDtypeStruct((B, S, ATT_WIDTH), BF16),
        scratch_shapes=[
            pltpu.VMEM((HEADS_PER_BLOCK, blk, blk), F32),
            pltpu.VMEM((HEADS_PER_BLOCK, blk, blk), F32),
            pltpu.VMEM((HEADS_PER_BLOCK, 1, blk), F32),
            pltpu.VMEM((HEADS_PER_BLOCK, 1, blk), F32),
            pltpu.VMEM((HEADS_PER_BLOCK, 1, blk), F32),
            pltpu.VMEM((HEADS_PER_BLOCK, VT_ROWS, blk), F32),
        ],
        compiler_params=_cparams(("parallel", "parallel", "arbitrary")),
        name="fox_attention",
    )(qq, kk, vt)


def _lru_kernel(x_ref, g_ref, cw_ref, cb_ref, wri_ref, bri_ref, lam_ref, o_ref,
                ext_ref, a_ref, u_ref, h_ref):
    ts = x_ref.shape[1]
    W = x_ref.shape[2]

    @pl.when(pl.program_id(1) == 0)
    def _():
        ext_ref[0:SUBLANES, :] = jnp.zeros((SUBLANES, W), F32)
        h_ref[...] = jnp.zeros_like(h_ref)

    x = x_ref[0]
    ext_ref[SUBLANES:SUBLANES + ts, :] = x
    xc = cb_ref[...] + cw_ref[CONV_W - 1:CONV_W, :] * x
    for back in range(1, CONV_W):
        xc = xc + cw_ref[CONV_W - 1 - back:CONV_W - back, :] * ext_ref[SUBLANES - back:SUBLANES - back + ts, :]
    ext_ref[0:SUBLANES, :] = x[ts - SUBLANES:ts, :]

    ri = jnp.dot(xc.astype(BF16), wri_ref[...], preferred_element_type=F32) + bri_ref[...]
    r = jax.nn.sigmoid(ri[:, :W])
    i = jax.nn.sigmoid(ri[:, W:])
    neg_lam = -lam_ref[...]
    softplus = jnp.maximum(neg_lam, 0.0) + jnp.log1p(jnp.exp(-jnp.abs(neg_lam)))
    a = jnp.exp(-LRU_C * r * softplus)
    a_ref[...] = a
    u_ref[...] = jnp.sqrt(jnp.maximum(1.0 - a * a, 0.0)) * (i * xc)

    row = lax.broadcasted_iota(jnp.int32, (SUBLANES, W), 0)

    def group(j, h_prev):
        off = pl.multiple_of(j * SUBLANES, SUBLANES)
        A = a_ref[pl.ds(off, SUBLANES), :]
        U = u_ref[pl.ds(off, SUBLANES), :]
        for d in (1, 2, 4):
            keep = row >= d
            U = U + A * jnp.where(keep, pltpu.roll(U, d, axis=0), 0.0)
            A = A * jnp.where(keep, pltpu.roll(A, d, axis=0), 1.0)
        H = A * h_prev + U
        u_ref[pl.ds(off, SUBLANES), :] = H
        return jnp.broadcast_to(H[SUBLANES - 1:SUBLANES, :], (SUBLANES, W))

    h_ref[...] = lax.fori_loop(0, ts // SUBLANES, group, h_ref[...])

    gate = g_ref[0]
    gelu = 0.5 * gate * (1.0 + jnp.tanh(0.7978845608028654 * (gate + 0.044715 * (gate * gate * gate))))
    o_ref[0] = (gelu * u_ref[...]).astype(o_ref.dtype)


def _lru(xg, cw, cb, wri, bri, lam, ts):
    B, S, _ = xg.shape
    W = LRU_WIDTH
    const = lambda b, s: (0, 0)
    return pl.pallas_call(
        _lru_kernel,
        grid=(B, S // ts),
        in_specs=[
            pl.BlockSpec((1, ts, W), lambda b, s: (b, s, 0)),
            pl.BlockSpec((1, ts, W), lambda b, s: (b, s, 1)),
            pl.BlockSpec(cw.shape, const),
            pl.BlockSpec(cb.shape, const),
            pl.BlockSpec(wri.shape, const),
            pl.BlockSpec(bri.shape, const),
            pl.BlockSpec(lam.shape, const),
        ],
        out_specs=pl.BlockSpec((1, ts, W), lambda b, s: (b, s, 0)),
        out_shape=jax.ShapeDtypeStruct((B, S, W), BF16),
        scratch_shapes=[
            pltpu.VMEM((ts + SUBLANES, W), F32),
            pltpu.VMEM((ts, W), F32),
            pltpu.VMEM((ts, W), F32),
            pltpu.VMEM((SUBLANES, W), F32),
        ],
        compiler_params=_cparams(("parallel", "arbitrary")),
        name="rg_lru",
    )(xg, xg, cw, cb, wri, bri, lam)


def _layer_norm(y, g, b):
    mu = jnp.mean(y, axis=-1, keepdims=True)
    yc = y - mu
    var = jnp.mean(yc * yc, axis=-1, keepdims=True)
    return yc * lax.rsqrt(var + LN_EPS) * g + b


def _route(logits_t):
    mx = jnp.max(logits_t, axis=0, keepdims=True)
    e = jnp.exp(logits_t - mx)
    p = e / jnp.sum(e, axis=0, keepdims=True)
    v = [p[SUBLANES * j:SUBLANES * (j + 1), :] for j in range(EXPERTS_PER_GROUP)]

    def first_max(vals):
        best = functools.reduce(jnp.maximum, vals)
        idx = jnp.full(best.shape, len(vals) - 1, jnp.int32)
        for j in range(len(vals) - 2, -1, -1):
            idx = jnp.where(vals[j] == best, j, idx)
        return best, idx

    m1, idx1 = first_max(v)
    m2, idx2 = first_max([jnp.where(idx1 == j, -1.0, v[j]) for j in range(EXPERTS_PER_GROUP)])
    grp = lax.broadcasted_iota(jnp.int32, m1.shape, 0)
    real = grp < N_GROUPS
    score = jnp.where(real, m1 + m2, -1.0)
    best = jnp.max(score, axis=0, keepdims=True)
    chosen = jnp.min(jnp.where(score == best, grp, SUBLANES), axis=0, keepdims=True)
    sel = grp == chosen
    den = jnp.where(real, m1 + m2, 1.0)
    g1 = m1 / den
    g2 = m2 / den
    return [jnp.where(sel, jnp.where(idx1 == j, g1, 0.0) + jnp.where(idx2 == j, g2, 0.0), 0.0)
            for j in range(EXPERTS_PER_GROUP)]


def _mix_kernel(x_ref, attn_ref, rec_ref, ng_ref, wout_ref, lng_ref, lnb_ref, wr_ref, br_ref,
                x1_ref, x1b_ref, comb_ref, *, alpha):
    tm = x_ref.shape[0]
    ya = attn_ref[...].astype(F32)
    yr = rec_ref[...].astype(F32)
    ya = ya * lax.rsqrt(jnp.mean(ya * ya, axis=-1, keepdims=True) + RMS_EPS) * ng_ref[:, :ATT_WIDTH]
    yr = yr * lax.rsqrt(jnp.mean(yr * yr, axis=-1, keepdims=True) + RMS_EPS) * ng_ref[:, ATT_WIDTH:]
    out = jnp.dot(ya.astype(BF16), wout_ref[:ATT_WIDTH, :], preferred_element_type=F32)
    out = out + jnp.dot(yr.astype(BF16), wout_ref[ATT_WIDTH:, :], preferred_element_type=F32)
    x1 = _layer_norm(alpha * x_ref[...] + out, lng_ref[...], lnb_ref[...])
    x1_ref[...] = x1
    x1_hi = x1.astype(BF16)
    x1b_ref[...] = x1_hi

    x1_lo = (x1 - x1_hi.astype(F32)).astype(BF16)
    nt = (((1,), (1,)), ((), ()))
    o = (lax.dot_general(wr_ref[...], x1_hi, nt, preferred_element_type=F32)
         + lax.dot_general(wr_ref[...], x1_lo, nt, preferred_element_type=F32))
    rows = 4 * SUBLANES
    logits_t = o[:rows, :] + o[rows:, :] + br_ref[...]
    comb_t = _route(logits_t)
    comb_t = jnp.concatenate(comb_t + [jnp.zeros((LANES - rows, tm), F32)], axis=0)
    comb_ref[...] = comb_t.T


def _mix_out(x2d, attn2d, rec2d, ng, wout, lng, lnb, wr, br, alpha, tm):
    T, D = x2d.shape
    const = lambda i: (0, 0)
    tile = lambda i: (i, 0)
    return pl.pallas_call(
        functools.partial(_mix_kernel, alpha=alpha),
        grid=(T // tm,),
        in_specs=[
            pl.BlockSpec((tm, D), tile),
            pl.BlockSpec((tm, ATT_WIDTH), tile),
            pl.BlockSpec((tm, LRU_WIDTH), tile),
            pl.BlockSpec(ng.shape, const),
            pl.BlockSpec(wout.shape, const),
            pl.BlockSpec(lng.shape, const),
            pl.BlockSpec(lnb.shape, const),
            pl.BlockSpec(wr.shape, const),
            pl.BlockSpec(br.shape, const),
        ],
        out_specs=[
            pl.BlockSpec((tm, D), tile),
            pl.BlockSpec((tm, D), tile),
            pl.BlockSpec((tm, LANES), tile),
        ],
        out_shape=[
            jax.ShapeDtypeStruct((T, D), F32),
            jax.ShapeDtypeStruct((T, D), BF16),
            jax.ShapeDtypeStruct((T, LANES), F32),
        ],
        compiler_params=_cparams(("parallel",)),
        name="mix_out_router",
    )(x2d, attn2d, rec2d, ng, wout, lng, lnb, wr, br)


def _moe_kernel(x1_ref, xb_ref, comb_ref, wgu_ref, wd_ref, lng_ref, lnb_ref, o_ref, *, alpha):
    g = pl.program_id(1)
    F = wd_ref.shape[1]

    @pl.when(g == 0)
    def _():
        o_ref[...] = jnp.zeros_like(o_ref)

    xb = xb_ref[...]
    comb = comb_ref[...]
    lane = lax.broadcasted_iota(jnp.int32, comb.shape, 1)
    acc = o_ref[...]
    for j in range(EXPERTS_PER_GROUP):
        gu = jnp.dot(xb, wgu_ref[j], preferred_element_type=F32)
        gate = gu[:, :F]
        w = jnp.sum(jnp.where(lane == SUBLANES * j + g, comb, 0.0), axis=1, keepdims=True)
        act = gate * jax.nn.sigmoid(gate) * gu[:, F:] * w
        acc = acc + jnp.dot(act.astype(BF16), wd_ref[j], preferred_element_type=F32)
    o_ref[...] = acc

    @pl.when(g == N_GROUPS - 1)
    def _():
        o_ref[...] = _layer_norm(alpha * x1_ref[...] + o_ref[...], lng_ref[...], lnb_ref[...])


def _moe(x1, x1b, comb, wgu, wd, lng, lnb, alpha, tm):
    T, D = x1.shape
    F = wd.shape[1]
    const = lambda i, g: (0, 0)
    tile = lambda i, g: (i, 0)
    return pl.pallas_call(
        functools.partial(_moe_kernel, alpha=alpha),
        grid=(T // tm, N_GROUPS),
        in_specs=[
            pl.BlockSpec((tm, D), tile),
            pl.BlockSpec((tm, D), tile),
            pl.BlockSpec((tm, LANES), tile),
            pl.BlockSpec((EXPERTS_PER_GROUP, D, 2 * F), lambda i, g: (g, 0, 0)),
            pl.BlockSpec((EXPERTS_PER_GROUP, F, D), lambda i, g: (g, 0, 0)),
            pl.BlockSpec(lng.shape, const),
            pl.BlockSpec(lnb.shape, const),
        ],
        out_specs=pl.BlockSpec((tm, D), tile),
        out_shape=jax.ShapeDtypeStruct((T, D), F32),
        compiler_params=_cparams(("parallel", "arbitrary")),
        name="moe_ffn",
    )(x1, x1b, comb, wgu, wd, lng, lnb)


def _block_diag(w):
    n, c, d = w.shape
    return jnp.einsum('ncd,nm->ncmd', w, jnp.eye(n, dtype=w.dtype)).reshape(n * c, n * d)


def _router_operands(w_router, b_router):
    D = w_router.shape[0]
    wt = w_router.T.reshape(N_GROUPS, EXPERTS_PER_GROUP, D).transpose(1, 0, 2)
    wt = jnp.pad(wt, ((0, 0), (0, SUBLANES - N_GROUPS), (0, 0))).reshape(4 * SUBLANES, D)
    hi = wt.astype(BF16)
    lo = (wt - hi.astype(F32)).astype(BF16)
    bt = b_router.reshape(N_GROUPS, EXPERTS_PER_GROUP).T
    bt = jnp.pad(bt, ((0, 0), (0, SUBLANES - N_GROUPS)), constant_values=NEG_INF).reshape(4 * SUBLANES, 1)
    return jnp.concatenate([hi, lo], axis=0), bt.astype(F32)


def kernel(x, w_in, b_forget, conv_w, conv_b, w_rec_gate, b_rec_gate, w_in_gate, b_in_gate, lru_lambda,
           mix_norm_g, w_out, ln1_g, ln1_b, w_router, b_router, w_exp_gate, w_exp_up, w_exp_down,
           ln2_g, ln2_b):
    B, S, D = x.shape
    depth = w_in.shape[0]
    alpha = (2.0 * depth) ** 0.25
    T = B * S
    o_f = 3 * ATT_WIDTH
    o_x = o_f + N_HEADS

    blk_attn = min(512, S)
    ts_lru = min(512, S)
    tm_mix = min(512, T)
    tm_moe = min(1024, T)

    wr, br = _router_operands(w_router, b_router)
    row = lambda a: a.reshape(1, -1).astype(F32)

    for l in range(depth):
        wqkv = w_in[l, :, :o_f].astype(BF16)
        wf = jnp.pad(w_in[l, :, o_f:o_x], ((0, 0), (0, LANES - N_HEADS))).astype(BF16)
        bf = jnp.pad(b_forget[l], (0, LANES - N_HEADS)).reshape(1, LANES).astype(F32)
        wxg = w_in[l, :, o_x:].astype(BF16)
        qq, kk, vt, xg = _in_proj(x, wqkv, wf, bf, wxg, blk_attn)

        attn = _attention(qq, kk, vt, blk_attn)

        wri = jnp.concatenate([_block_diag(w_rec_gate[l]), _block_diag(w_in_gate[l])], axis=1).astype(BF16)
        bri = jnp.concatenate([b_rec_gate[l], b_in_gate[l]]).reshape(1, -1).astype(F32)
        rec = _lru(xg, conv_w[l].astype(F32), row(conv_b[l]), wri, bri, row(lru_lambda[l]), ts_lru)

        x1, x1b, comb = _mix_out(
            x.reshape(T, D), attn.reshape(T, ATT_WIDTH), rec.reshape(T, LRU_WIDTH), row(mix_norm_g[l]),
            w_out[l].astype(BF16), row(ln1_g[l]), row(ln1_b[l]), wr, br, alpha, tm_mix)

        wgu = jnp.concatenate([w_exp_gate[l], w_exp_up[l]], axis=-1).astype(BF16)
        x = _moe(x1, x1b, comb, wgu, w_exp_down[l].astype(BF16), row(ln2_g[l]), row(ln2_b[l]),
                 alpha, tm_moe).reshape(B, S, D).astype(x.dtype)
    return x
```

```python
import functools

import jax
import jax.numpy as jnp
from jax import lax
from jax.experimental import pallas as pl
from jax.experimental.pallas import tpu as pltpu

F32 = jnp.float32
BF16 = jnp.bfloat16

HEAD_DIM = 64
N_HEADS = 8
ATT_WIDTH = N_HEADS * HEAD_DIM
LRU_WIDTH = 512
LRU_C = 8.0
CONV_W = 4
N_GROUPS = 4
EXPERTS_PER_GROUP = 4
N_EXPERTS = N_GROUPS * EXPERTS_PER_GROUP
LN_EPS = 1e-5
RMS_EPS = 1e-6
NEG_INF = -1e30

LANES = 128
SUBLANES = 8
HEADS_PER_BLOCK = LANES // HEAD_DIM
N_HEAD_BLOCKS = N_HEADS // HEADS_PER_BLOCK
BIAS_SLOT = 8
BF16_SUBLANES = 16
VT_ROWS = HEAD_DIM + BF16_SUBLANES
LOG2E = 1.4426950408889634
VMEM_LIMIT = 48 * 1024 * 1024
MIX_CHUNK = 512
MOE_CHUNK = 288
ROUTE_ROWS = 4 * SUBLANES


def _cparams(sem):
    return pltpu.CompilerParams(dimension_semantics=sem, vmem_limit_bytes=VMEM_LIMIT)


def _in_proj_kernel(x_ref, wq_ref, wk_ref, wvt_ref, vt_one_ref, wf_ref, bf_ref, wxg_ref,
                    qq_ref, kk_ref, vt_ref, xg_ref, carry_ref):
    tm = x_ref.shape[1]

    @pl.when(pl.program_id(1) == 0)
    def _():
        carry_ref[...] = jnp.zeros_like(carry_ref)

    xb = x_ref[0].astype(BF16)
    f = jnp.dot(xb, wf_ref[...], preferred_element_type=F32) + bf_ref[...]

    c = jnp.minimum(f, 0.0) - jnp.log1p(jnp.exp(-jnp.abs(f)))
    row = lax.broadcasted_iota(jnp.int32, c.shape, 0)
    d = 1
    while d < tm:
        c = c + jnp.where(row >= d, pltpu.roll(c, d, axis=0), 0.0)
        d *= 2
    c = c + carry_ref[...]
    carry_ref[...] = c[tm - 1:tm, :]

    c2 = c * LOG2E
    parts = []
    for h in range(N_HEADS):
        col = c2[:, h:h + 1]
        hi = col.astype(BF16).astype(F32)
        rem = col - hi
        mid = rem.astype(BF16).astype(F32)
        parts.append((hi, mid, rem - mid))

    lane = lax.broadcasted_iota(jnp.int32, (tm, LANES), 1)
    slot = lane % BIAS_SLOT
    head_slot = lane // BIAS_SLOT
    valid = head_slot < HEADS_PER_BLOCK
    for hb in range(N_HEAD_BLOCKS):
        def part(k):
            return jnp.where(head_slot == 0, parts[2 * hb][k], parts[2 * hb + 1][k])

        cq = jnp.where(slot == 0, part(0), jnp.where(slot == 1, part(1), part(2)))
        ck = jnp.where(slot == 3, part(0), jnp.where(slot == 4, part(1), part(2)))
        qx = jnp.where(valid & (slot < 3), cq, jnp.where(valid & (slot < 6), 1.0, 0.0))
        kx = jnp.where(valid & (slot < 3), 1.0, jnp.where(valid & (slot < 6), -ck, 0.0))
        qq_ref[0, :, (2 * hb + 1) * LANES:(2 * hb + 2) * LANES] = qx.astype(BF16)
        kk_ref[0, :, (2 * hb + 1) * LANES:(2 * hb + 2) * LANES] = kx.astype(BF16)

    vt = lax.dot_general(wvt_ref[...], xb, (((1,), (1,)), ((), ())), preferred_element_type=F32)
    vt_ref[0, 0] = (vt + vt_one_ref[...]).astype(BF16)

    q = jnp.dot(xb, wq_ref[...], preferred_element_type=F32) * (HEAD_DIM ** -0.5 * LOG2E)
    k = jnp.dot(xb, wk_ref[...], preferred_element_type=F32)
    for hb in range(N_HEAD_BLOCKS):
        qq_ref[0, :, 2 * hb * LANES:(2 * hb + 1) * LANES] = q[:, hb * LANES:(hb + 1) * LANES].astype(BF16)
        kk_ref[0, :, 2 * hb * LANES:(2 * hb + 1) * LANES] = k[:, hb * LANES:(hb + 1) * LANES].astype(BF16)
    xg_ref[0] = jnp.dot(xb, wxg_ref[...], preferred_element_type=F32)


def _in_proj(x, wq, wk, wvt, vt_one, wf, bf, wxg, tm):
    B, S, D = x.shape
    const = lambda b, s: (0, 0)
    tile = lambda b, s: (b, s, 0)
    return pl.pallas_call(
        _in_proj_kernel,
        grid=(B, S // tm),
        in_specs=[
            pl.BlockSpec((1, tm, D), tile),
            pl.BlockSpec(wq.shape, const),
            pl.BlockSpec(wk.shape, const),
            pl.BlockSpec(wvt.shape, const),
            pl.BlockSpec(vt_one.shape, const),
            pl.BlockSpec(wf.shape, const),
            pl.BlockSpec(bf.shape, const),
            pl.BlockSpec(wxg.shape, const),
        ],
        out_specs=[
            pl.BlockSpec((1, tm, 2 * ATT_WIDTH), tile),
            pl.BlockSpec((1, tm, 2 * ATT_WIDTH), tile),
            pl.BlockSpec((1, 1, N_HEADS * VT_ROWS, tm), lambda b, s: (b, s, 0, 0)),
            pl.BlockSpec((1, tm, 2 * LRU_WIDTH), tile),
        ],
        out_shape=[
            jax.ShapeDtypeStruct((B, S, 2 * ATT_WIDTH), BF16),
            jax.ShapeDtypeStruct((B, S, 2 * ATT_WIDTH), BF16),
            jax.ShapeDtypeStruct((B, S // tm, N_HEADS * VT_ROWS, tm), BF16),
            jax.ShapeDtypeStruct((B, S, 2 * LRU_WIDTH), F32),
        ],
        scratch_shapes=[pltpu.VMEM((1, LANES), F32)],
        compiler_params=_cparams(("parallel", "arbitrary")),
        name="in_proj",
    )(x, wq, wk, wvt, vt_one, wf, bf, wxg)


def _attn_kernel(qq_ref, kk_ref, vt_ref, o_ref, st0_ref, st1_ref, bm0_ref, bm1_ref, m_ref, acc_ref, *, blk):
    qi = pl.program_id(2)
    st_refs = (st0_ref, st1_ref)
    bm_refs = (bm0_ref, bm1_ref)
    q = qq_ref[0]
    lane = lax.broadcasted_iota(jnp.int32, q.shape, 1)
    zero = jnp.zeros_like(q)
    q_heads = []
    for h in range(HEADS_PER_BLOCK):
        own = ((lane >= h * HEAD_DIM) & (lane < (h + 1) * HEAD_DIM)) | (
            (lane >= LANES + h * BIAS_SLOT) & (lane < LANES + (h + 1) * BIAS_SLOT))
        q_heads.append(jnp.where(own, q, zero))

    m_ref[...] = jnp.full_like(m_ref, NEG_INF)
    acc_ref[...] = jnp.zeros_like(acc_ref)

    def scores(ki, h, slot, masked):
        k_blk = kk_ref[0, pl.ds(pl.multiple_of(ki * blk, blk), blk), :]
        st = lax.dot_general(k_blk, q_heads[h], (((1,), (1,)), ((), ())),
                             preferred_element_type=F32)
        if masked:
            key = lax.broadcasted_iota(jnp.int32, (blk, blk), 0)
            qry = lax.broadcasted_iota(jnp.int32, (blk, blk), 1)
            st = jnp.where(key - qry <= (qi - ki) * blk, st, NEG_INF)
        st_refs[slot][h] = st
        bm_refs[slot][h] = jnp.max(st, axis=0, keepdims=True)

    def update(ki, h, slot):
        m_old = m_ref[h]
        m_new = jnp.maximum(m_old, bm_refs[slot][h])
        alpha = jnp.exp2(m_old - m_new)
        p = jnp.exp2(st_refs[slot][h] - m_new).astype(BF16)
        vt = vt_ref[0, ki, h * VT_ROWS:(h + 1) * VT_ROWS, :]
        acc_ref[h] = alpha * acc_ref[h] + jnp.dot(vt, p, preferred_element_type=F32)
        m_ref[h] = m_new

    def stage(score_blk, score_slot, masked, update_blk, update_slot):
        for h in range(HEADS_PER_BLOCK):
            if score_blk is not None:
                scores(score_blk, h, score_slot, masked)
            if update_blk is not None:
                update(update_blk, h, update_slot)

    stage(0, 0, True, None, None)

    def body(j, carry):
        i = 2 * j
        stage(i + 1, 1, False, i, 0)
        stage(i + 2, 0, False, i + 1, 1)
        return carry

    lax.fori_loop(0, (qi - 1) // 2, body, 0)

    @pl.when(qi == 0)
    def _():
        stage(None, None, False, 0, 0)

    @pl.when(qi % 2 == 1)
    def _():
        stage(qi, 1, True, qi - 1, 0)
        stage(None, None, False, qi, 1)

    @pl.when((qi % 2 == 0) & (qi > 0))
    def _():
        stage(qi - 1, 1, False, qi - 2, 0)
        stage(qi, 0, True, qi - 1, 1)
        stage(None, None, False, qi, 0)

    ot = jnp.concatenate([acc_ref[h, :HEAD_DIM, :] / acc_ref[h, HEAD_DIM:HEAD_DIM + 1, :]
                          for h in range(HEADS_PER_BLOCK)], axis=0)
    o_ref[0] = ot.T.astype(o_ref.dtype)


def _attention(qq, kk, vt, blk):
    B, nblk, _, _ = vt.shape
    S = nblk * blk
    return pl.pallas_call(
        functools.partial(_attn_kernel, blk=blk),
        grid=(B, N_HEAD_BLOCKS, nblk),
        in_specs=[
            pl.BlockSpec((1, blk, 2 * LANES), lambda b, h, i: (b, i, h)),
            pl.BlockSpec((1, S, 2 * LANES), lambda b, h, i: (b, 0, h)),
            pl.BlockSpec((1, nblk, HEADS_PER_BLOCK * VT_ROWS, blk), lambda b, h, i: (b, 0, h, 0)),
        ],
        out_specs=pl.BlockSpec((1, blk, LANES), lambda b, h, i: (b, i, h)),
        out_shape=jax.ShapeDtypeStruct((B, S, ATT_WIDTH), BF16),
        scratch_shapes=[
            pltpu.VMEM((HEADS_PER_BLOCK, blk, blk), F32),
            pltpu.VMEM((HEADS_PER_BLOCK, blk, blk), F32),
            pltpu.VMEM((HEADS_PER_BLOCK, 1, blk), F32),
            pltpu.VMEM((HEADS_PER_BLOCK, 1, blk), F32),
            pltpu.VMEM((HEADS_PER_BLOCK, 1, blk), F32),
            pltpu.VMEM((HEADS_PER_BLOCK, VT_ROWS, blk), F32),
        ],
        compiler_params=_cparams(("parallel", "parallel", "arbitrary")),
        name="fox_attention",
    )(qq, kk, vt)


def _lru_kernel(x_ref, g_ref, cw_ref, cb_ref, wri_ref, bri_ref, lam_ref, o_ref,
                ext_ref, a_ref, u_ref, h_ref):
    ts = x_ref.shape[1]
    W = x_ref.shape[2]

    @pl.when(pl.program_id(1) == 0)
    def _():
        ext_ref[0:SUBLANES, :] = jnp.zeros((SUBLANES, W), F32)
        h_ref[...] = jnp.zeros_like(h_ref)

    x = x_ref[0]
    ext_ref[SUBLANES:SUBLANES + ts, :] = x
    xc = cb_ref[...] + cw_ref[CONV_W - 1:CONV_W, :] * x
    for back in range(1, CONV_W):
        xc = xc + cw_ref[CONV_W - 1 - back:CONV_W - back, :] * ext_ref[SUBLANES - back:SUBLANES - back + ts, :]
    ext_ref[0:SUBLANES, :] = x[ts - SUBLANES:ts, :]

    ri = jnp.dot(xc.astype(BF16), wri_ref[...], preferred_element_type=F32) + bri_ref[...]
    r = jax.nn.sigmoid(ri[:, :W])
    i = jax.nn.sigmoid(ri[:, W:])
    neg_lam = -lam_ref[...]
    softplus = jnp.maximum(neg_lam, 0.0) + jnp.log1p(jnp.exp(-jnp.abs(neg_lam)))
    a = jnp.exp(-LRU_C * r * softplus)
    a_ref[...] = a
    u_ref[...] = jnp.sqrt(jnp.maximum(1.0 - a * a, 0.0)) * (i * xc)

    row = lax.broadcasted_iota(jnp.int32, (SUBLANES, W), 0)

    def group(j, h_prev):
        off = pl.multiple_of(j * SUBLANES, SUBLANES)
        A = a_ref[pl.ds(off, SUBLANES), :]
        U = u_ref[pl.ds(off, SUBLANES), :]
        for d in (1, 2, 4):
            keep = row >= d
            U = U + A * jnp.where(keep, pltpu.roll(U, d, axis=0), 0.0)
            A = A * jnp.where(keep, pltpu.roll(A, d, axis=0), 1.0)
        H = A * h_prev + U
        u_ref[pl.ds(off, SUBLANES), :] = H
        return jnp.broadcast_to(H[SUBLANES - 1:SUBLANES, :], (SUBLANES, W))

    h_ref[...] = lax.fori_loop(0, ts // SUBLANES, group, h_ref[...])

    gate = g_ref[0]
    gelu = 0.5 * gate * (1.0 + jnp.tanh(0.7978845608028654 * (gate + 0.044715 * (gate * gate * gate))))
    o_ref[0] = (gelu * u_ref[...]).astype(o_ref.dtype)


def _lru(xg, cw, cb, wri, bri, lam, ts):
    B, S, _ = xg.shape
    W = LRU_WIDTH
    const = lambda b, s: (0, 0)
    return pl.pallas_call(
        _lru_kernel,
        grid=(B, S // ts),
        in_specs=[
            pl.BlockSpec((1, ts, W), lambda b, s: (b, s, 0)),
            pl.BlockSpec((1, ts, W), lambda b, s: (b, s, 1)),
            pl.BlockSpec(cw.shape, const),
            pl.BlockSpec(cb.shape, const),
            pl.BlockSpec(wri.shape, const),
            pl.BlockSpec(bri.shape, const),
            pl.BlockSpec(lam.shape, const),
        ],
        out_specs=pl.BlockSpec((1, ts, W), lambda b, s: (b, s, 0)),
        out_shape=jax.ShapeDtypeStruct((B, S, W), BF16),
        scratch_shapes=[
            pltpu.VMEM((ts + SUBLANES, W), F32),
            pltpu.VMEM((ts, W), F32),
            pltpu.VMEM((ts, W), F32),
            pltpu.VMEM((SUBLANES, W), F32),
        ],
        compiler_params=_cparams(("parallel", "arbitrary")),
        name="rg_lru",
    )(xg, xg, cw, cb, wri, bri, lam)


def _layer_norm(y, g, b):
    mu = jnp.mean(y, axis=-1, keepdims=True)
    yc = y - mu
    var = jnp.mean(yc * yc, axis=-1, keepdims=True)
    return yc * lax.rsqrt(var + LN_EPS) * g + b


def _route(logits_t):
    mx = jnp.max(logits_t, axis=0, keepdims=True)
    e = jnp.exp(logits_t - mx)
    p = e / jnp.sum(e, axis=0, keepdims=True)
    v = [p[SUBLANES * j:SUBLANES * (j + 1), :] for j in range(EXPERTS_PER_GROUP)]

    def first_max(vals):
        best = functools.reduce(jnp.maximum, vals)
        idx = jnp.full(best.shape, len(vals) - 1, jnp.int32)
        for j in range(len(vals) - 2, -1, -1):
            idx = jnp.where(vals[j] == best, j, idx)
        return best, idx

    m1, idx1 = first_max(v)
    m2, idx2 = first_max([jnp.where(idx1 == j, -1.0, v[j]) for j in range(EXPERTS_PER_GROUP)])
    grp = lax.broadcasted_iota(jnp.int32, m1.shape, 0)
    real = grp < N_GROUPS
    score = jnp.where(real, m1 + m2, -1.0)
    best = jnp.max(score, axis=0, keepdims=True)
    chosen = jnp.min(jnp.where(score == best, grp, SUBLANES), axis=0, keepdims=True)
    sel = grp == chosen
    den = jnp.where(real, m1 + m2, 1.0)
    g1 = m1 / den
    g2 = m2 / den
    comb = [jnp.where(sel, jnp.where(idx1 == j, g1, 0.0) + jnp.where(idx2 == j, g2, 0.0), 0.0)
            for j in range(EXPERTS_PER_GROUP)]
    return comb, chosen


def _sort_by_group(chosen, tri_ref):
    tm = chosen.shape[1]
    grp = lax.broadcasted_iota(jnp.int32, (SUBLANES, tm), 0)
    member = grp == chosen
    onehot = jnp.where(member, 1.0, 0.0).astype(BF16)
    count = jnp.dot(onehot, tri_ref[...], preferred_element_type=F32)
    total = count[:, tm - 1:tm]
    gcol = lax.broadcasted_iota(jnp.int32, (SUBLANES, 1), 0)
    start = jnp.zeros((SUBLANES, 1), F32)
    for k in range(N_GROUPS - 1):
        start = start + jnp.where(gcol > k, total[k:k + 1, :], 0.0)
    pos = jnp.sum(jnp.where(member, start + count - 1.0, 0.0), axis=0, keepdims=True)
    return pos, start, start + total


def _mix_kernel(x_ref, attn_ref, rec_ref, ng_ref, wout_ref, lng_ref, lnb_ref, wr_ref, br_ref, tri_ref,
                x1_ref, x1b_ref, comb_ref, pos_ref, seg_ref, *, alpha):
    tm = x_ref.shape[0]
    chunk = min(MIX_CHUNK, tm)
    rows = ROUTE_ROWS
    nt = (((1,), (1,)), ((), ()))
    comb_parts, chosen_parts = [], []
    for r0 in range(0, tm, chunk):
        rs = slice(r0, r0 + chunk)
        ya = attn_ref[rs, :].astype(F32)
        yr = rec_ref[rs, :].astype(F32)
        ya = ya * lax.rsqrt(jnp.mean(ya * ya, axis=-1, keepdims=True) + RMS_EPS) * ng_ref[:, :ATT_WIDTH]
        yr = yr * lax.rsqrt(jnp.mean(yr * yr, axis=-1, keepdims=True) + RMS_EPS) * ng_ref[:, ATT_WIDTH:]
        out = jnp.dot(ya.astype(BF16), wout_ref[:ATT_WIDTH, :], preferred_element_type=F32)
        out = out + jnp.dot(yr.astype(BF16), wout_ref[ATT_WIDTH:, :], preferred_element_type=F32)
        x1 = _layer_norm(alpha * x_ref[rs, :] + out, lng_ref[...], lnb_ref[...])
        x1_ref[rs, :] = x1
        x1_hi = x1.astype(BF16)
        x1b_ref[rs, :] = x1_hi

        x1_lo = (x1 - x1_hi.astype(F32)).astype(BF16)
        o = (lax.dot_general(wr_ref[...], x1_hi, nt, preferred_element_type=F32)
             + lax.dot_general(wr_ref[...], x1_lo, nt, preferred_element_type=F32))
        logits_t = o[:rows, :] + o[rows:, :] + br_ref[...]
        comb_t, chosen = _route(logits_t)
        comb_parts.append(jnp.concatenate(comb_t, axis=0))
        chosen_parts.append(chosen)

    chosen = jnp.concatenate(chosen_parts, axis=1)
    pos, seg_start, seg_end = _sort_by_group(chosen, tri_ref)
    pos_ref[0] = pos
    lane = lax.broadcasted_iota(jnp.int32, (SUBLANES, LANES), 1)
    seg_ref[0] = jnp.where(lane == 0, seg_start, jnp.where(lane == 1, seg_end, 0.0))
    meta_row = lax.broadcasted_iota(jnp.int32, (SUBLANES, tm), 0)
    meta = jnp.where(meta_row == 0, chosen.astype(F32), jnp.where(meta_row == 1, pos, 0.0))
    comb_t = jnp.concatenate([jnp.concatenate(comb_parts, axis=1), meta,
                              jnp.zeros((LANES - rows - SUBLANES, tm), F32)], axis=0)
    comb_ref[...] = comb_t.T


def _mix_out(x2d, attn2d, rec2d, ng, wout, lng, lnb, wr, br, tri, alpha, tm):
    T, D = x2d.shape
    const = lambda i: (0, 0)
    tile = lambda i: (i, 0)
    n_tiles = T // tm
    return pl.pallas_call(
        functools.partial(_mix_kernel, alpha=alpha),
        grid=(T // tm,),
        in_specs=[
            pl.BlockSpec((tm, D), tile),
            pl.BlockSpec((tm, ATT_WIDTH), tile),
            pl.BlockSpec((tm, LRU_WIDTH), tile),
            pl.BlockSpec(ng.shape, const),
            pl.BlockSpec(wout.shape, const),
            pl.BlockSpec(lng.shape, const),
            pl.BlockSpec(lnb.shape, const),
            pl.BlockSpec(wr.shape, const),
            pl.BlockSpec(br.shape, const),
            pl.BlockSpec(tri.shape, const),
        ],
        out_specs=[
            pl.BlockSpec((tm, D), tile),
            pl.BlockSpec((tm, D), tile),
            pl.BlockSpec((tm, LANES), tile),
            pl.BlockSpec((1, 1, tm), lambda i: (i, 0, 0)),
            pl.BlockSpec((1, SUBLANES, LANES), lambda i: (i, 0, 0)),
        ],
        out_shape=[
            jax.ShapeDtypeStruct((T, D), F32),
            jax.ShapeDtypeStruct((T, D), BF16),
            jax.ShapeDtypeStruct((T, LANES), F32),
            jax.ShapeDtypeStruct((n_tiles, 1, tm), F32),
            jax.ShapeDtypeStruct((n_tiles, SUBLANES, LANES), F32),
        ],
        compiler_params=_cparams(("parallel",)),
        name="mix_out_router",
    )(x2d, attn2d, rec2d, ng, wout, lng, lnb, wr, br, tri)


def _moe_kernel(seg_ref, x1_ref, xb_ref, comb_ref, pos_ref, wgu_ref, wd_ref, lng_ref, lnb_ref, o_ref,
                perm_ref, xs_ref, cs_ref, ys_ref, *, alpha):
    i = pl.program_id(0)
    g = pl.program_id(1)
    tm = xb_ref.shape[0]
    F = wd_ref.shape[1]

    @pl.when(g == 0)
    def _():
        pos = pos_ref[0].astype(jnp.int32)
        srow = lax.broadcasted_iota(jnp.int32, (tm, tm), 0)
        perm_ref[...] = jnp.where(srow == pos, 1.0, 0.0).astype(BF16)
        xs_ref[...] = jnp.dot(perm_ref[...], xb_ref[...], preferred_element_type=F32).astype(BF16)
        comb = comb_ref[...]
        hi = comb.astype(BF16)
        lo = (comb - hi.astype(F32)).astype(BF16)
        c2 = jnp.dot(perm_ref[...], jnp.concatenate([hi, lo], axis=1), preferred_element_type=F32)
        cs_ref[...] = c2[:, :LANES] + c2[:, LANES:]
        ys_ref[...] = jnp.zeros_like(ys_ref)

    start = seg_ref[(i * N_GROUPS + g) * 2]
    end = seg_ref[(i * N_GROUPS + g) * 2 + 1]
    base = (start // BF16_SUBLANES) * BF16_SUBLANES
    n_chunks = (end - base + MOE_CHUNK - 1) // MOE_CHUNK
    lane = lax.broadcasted_iota(jnp.int32, (MOE_CHUNK, LANES), 1)
    crow = lax.broadcasted_iota(jnp.int32, (MOE_CHUNK, 1), 0)

    def chunk(c, carry):
        lo_row = base + c * MOE_CHUNK
        r0 = pl.multiple_of(jnp.minimum(lo_row, tm - MOE_CHUNK), BF16_SUBLANES)
        rows = pl.ds(r0, MOE_CHUNK)
        xs = xs_ref[rows, :]
        cs = cs_ref[rows, :]
        grow = r0 + crow
        live = (grow >= jnp.maximum(start, lo_row)) & (grow < jnp.minimum(end, lo_row + MOE_CHUNK))
        acc = ys_ref[rows, :]
        for j in range(EXPERTS_PER_GROUP):
            gu = jnp.dot(xs, wgu_ref[j], preferred_element_type=F32)
            gate = gu[:, :F]
            w = jnp.sum(jnp.where(lane == SUBLANES * j + g, cs, 0.0), axis=1, keepdims=True)
            w = jnp.where(live, w, 0.0)
            act = gate * jax.nn.sigmoid(gate) * gu[:, F:] * w
            acc = acc + jnp.dot(act.astype(BF16), wd_ref[j], preferred_element_type=F32)
        ys_ref[rows, :] = acc
        return carry

    lax.fori_loop(0, n_chunks, chunk, 0)

    @pl.when(g == N_GROUPS - 1)
    def _():
        pos_col = comb_ref[:, ROUTE_ROWS + 1:ROUTE_ROWS + 2].astype(jnp.int32)
        scol = lax.broadcasted_iota(jnp.int32, (tm, tm), 1)
        perm_ref[...] = jnp.where(scol == pos_col, 1.0, 0.0).astype(BF16)
        y = jnp.dot(perm_ref[...], ys_ref[...].astype(BF16), preferred_element_type=F32)
        o_ref[...] = _layer_norm(alpha * x1_ref[...] + y, lng_ref[...], lnb_ref[...])


def _moe(seg, x1, x1b, comb, pos, wgu, wd, lng, lnb, alpha, tm):
    T, D = x1.shape
    F = wd.shape[1]
    const = lambda i, g, seg: (0, 0)
    tile = lambda i, g, seg: (i, 0)
    return pl.pallas_call(
        functools.partial(_moe_kernel, alpha=alpha),
        grid_spec=pltpu.PrefetchScalarGridSpec(
            num_scalar_prefetch=1,
            grid=(T // tm, N_GROUPS),
            in_specs=[
                pl.BlockSpec((tm, D), tile),
                pl.BlockSpec((tm, D), tile),
                pl.BlockSpec((tm, LANES), tile),
                pl.BlockSpec((1, 1, tm), lambda i, g, seg: (i, 0, 0)),
                pl.BlockSpec((EXPERTS_PER_GROUP, D, 2 * F), lambda i, g, seg: (g, 0, 0)),
                pl.BlockSpec((EXPERTS_PER_GROUP, F, D), lambda i, g, seg: (g, 0, 0)),
                pl.BlockSpec(lng.shape, const),
                pl.BlockSpec(lnb.shape, const),
            ],
            out_specs=pl.BlockSpec((tm, D), tile),
            scratch_shapes=[
                pltpu.VMEM((tm, tm), BF16),
                pltpu.VMEM((tm, D), BF16),
                pltpu.VMEM((tm, LANES), F32),
                pltpu.VMEM((tm, D), F32),
            ],
        ),
        out_shape=jax.ShapeDtypeStruct((T, D), F32),
        compiler_params=_cparams(("parallel", "arbitrary")),
        name="moe_ffn",
    )(seg, x1, x1b, comb, pos, wgu, wd, lng, lnb)


def _block_diag(w):
    n, c, d = w.shape
    return jnp.einsum('ncd,nm->ncmd', w, jnp.eye(n, dtype=w.dtype)).reshape(n * c, n * d)


def _router_operands(w_router, b_router):
    D = w_router.shape[0]
    wt = w_router.T.reshape(N_GROUPS, EXPERTS_PER_GROUP, D).transpose(1, 0, 2)
    wt = jnp.pad(wt, ((0, 0), (0, SUBLANES - N_GROUPS), (0, 0))).reshape(4 * SUBLANES, D)
    hi = wt.astype(BF16)
    lo = (wt - hi.astype(F32)).astype(BF16)
    bt = b_router.reshape(N_GROUPS, EXPERTS_PER_GROUP).T
    bt = jnp.pad(bt, ((0, 0), (0, SUBLANES - N_GROUPS)), constant_values=NEG_INF).reshape(4 * SUBLANES, 1)
    return jnp.concatenate([hi, lo], axis=0), bt.astype(F32)


def kernel(x, w_in, b_forget, conv_w, conv_b, w_rec_gate, b_rec_gate, w_in_gate, b_in_gate, lru_lambda,
           mix_norm_g, w_out, ln1_g, ln1_b, w_router, b_router, w_exp_gate, w_exp_up, w_exp_down,
           ln2_g, ln2_b):
    B, S, D = x.shape
    depth = w_in.shape[0]
    alpha = (2.0 * depth) ** 0.25
    T = B * S
    o_f = 3 * ATT_WIDTH
    o_x = o_f + N_HEADS

    blk_attn = min(512, S)
    ts_lru = min(512, S)
    tm_moe = min(1024, T)
    tri = jnp.triu(jnp.ones((tm_moe, tm_moe), F32)).astype(BF16)

    wr, br = _router_operands(w_router, b_router)
    row = lambda a: a.reshape(1, -1).astype(F32)
    vt_one = (jnp.arange(N_HEADS * VT_ROWS) % VT_ROWS == HEAD_DIM).astype(F32).reshape(-1, 1)

    for l in range(depth):
        wq = w_in[l, :, :ATT_WIDTH].astype(BF16)
        wk = w_in[l, :, ATT_WIDTH:2 * ATT_WIDTH].astype(BF16)
        wvt = w_in[l, :, 2 * ATT_WIDTH:o_f].T.reshape(N_HEADS, HEAD_DIM, D)
        wvt = jnp.pad(wvt, ((0, 0), (0, VT_ROWS - HEAD_DIM), (0, 0))).reshape(N_HEADS * VT_ROWS, D).astype(BF16)
        wf = jnp.pad(w_in[l, :, o_f:o_x], ((0, 0), (0, LANES - N_HEADS))).astype(BF16)
        bf = jnp.pad(b_forget[l], (0, LANES - N_HEADS)).reshape(1, LANES).astype(F32)
        wxg = w_in[l, :, o_x:].astype(BF16)
        qq, kk, vt, xg = _in_proj(x, wq, wk, wvt, vt_one, wf, bf, wxg, blk_attn)

        attn = _attention(qq, kk, vt, blk_attn)

        wri = jnp.concatenate([_block_diag(w_rec_gate[l]), _block_diag(w_in_gate[l])], axis=1).astype(BF16)
        bri = jnp.concatenate([b_rec_gate[l], b_in_gate[l]]).reshape(1, -1).astype(F32)
        rec = _lru(xg, conv_w[l].astype(F32), row(conv_b[l]), wri, bri, row(lru_lambda[l]), ts_lru)

        x1, x1b, comb, pos, seg = _mix_out(
            x.reshape(T, D), attn.reshape(T, ATT_WIDTH), rec.reshape(T, LRU_WIDTH), row(mix_norm_g[l]),
            w_out[l].astype(BF16), row(ln1_g[l]), row(ln1_b[l]), wr, br, tri, alpha, tm_moe)

        seg = seg[:, :N_GROUPS, :2].astype(jnp.int32).reshape(-1)
        wgu = jnp.concatenate([w_exp_gate[l], w_exp_up[l]], axis=-1).astype(BF16)
        x = _moe(seg, x1, x1b, comb, pos, wgu, w_exp_down[l].astype(BF16), row(ln2_g[l]), row(ln2_b[l]),
                 alpha, tm_moe).reshape(B, S, D).astype(x.dtype)
    return x
```

```python
import functools

import jax
import jax.numpy as jnp
from jax import lax
from jax.experimental import pallas as pl
from jax.experimental.pallas import tpu as pltpu

F32 = jnp.float32
BF16 = jnp.bfloat16

HEAD_DIM = 64
N_HEADS = 8
ATT_WIDTH = N_HEADS * HEAD_DIM
LRU_WIDTH = 512
LRU_C = 8.0
CONV_W = 4
N_GROUPS = 4
EXPERTS_PER_GROUP = 4
N_EXPERTS = N_GROUPS * EXPERTS_PER_GROUP
LN_EPS = 1e-5
RMS_EPS = 1e-6
NEG_INF = -1e30

LANES = 128
SUBLANES = 8
HEADS_PER_BLOCK = LANES // HEAD_DIM
N_HEAD_BLOCKS = N_HEADS // HEADS_PER_BLOCK
BIAS_SLOT = 8
BF16_SUBLANES = 16
VT_ROWS = HEAD_DIM + BF16_SUBLANES
LOG2E = 1.4426950408889634
GELU_K = 0.7978845608028654
VMEM_LIMIT = 48 * 1024 * 1024
MIX_CHUNK = 512
MOE_CHUNK = 288
PERM_ROWS = 256
ROUTE_ROWS = 4 * SUBLANES


def _cparams(sem):
    return pltpu.CompilerParams(dimension_semantics=sem, vmem_limit_bytes=VMEM_LIMIT)


def _lru_tile(x, gate, cw_ref, cb_ref, wri_ref, bri_ref, lam_ref, ext_ref, h_ref):
    ts, W = x.shape
    ext_ref[SUBLANES:SUBLANES + ts, :] = x
    xc = cb_ref[...] + cw_ref[CONV_W - 1:CONV_W, :] * x
    for back in range(1, CONV_W):
        xc = xc + cw_ref[CONV_W - 1 - back:CONV_W - back, :] * ext_ref[SUBLANES - back:SUBLANES - back + ts, :]
    ext_ref[0:SUBLANES, :] = x[ts - SUBLANES:ts, :]

    ri = jnp.dot(xc.astype(BF16), wri_ref[...], preferred_element_type=F32) + bri_ref[...]
    r = 0.5 * jnp.tanh(0.5 * ri[:, :W]) + 0.5
    i = 0.5 * jnp.tanh(0.5 * ri[:, W:]) + 0.5
    neg_lam = -lam_ref[...]
    softplus = jnp.maximum(neg_lam, 0.0) + jnp.log1p(jnp.exp(-jnp.abs(neg_lam)))
    a = jnp.exp(-LRU_C * r * softplus)
    v = jnp.maximum(1.0 - a * a, 0.0)
    u = jnp.where(v > 0.0, v * lax.rsqrt(v), 0.0) * (i * xc)

    row = lax.broadcasted_iota(jnp.int32, (SUBLANES, W), 0)
    h_prev = h_ref[...]
    hs = []
    for j in range(ts // SUBLANES):
        A = a[j * SUBLANES:(j + 1) * SUBLANES, :]
        U = u[j * SUBLANES:(j + 1) * SUBLANES, :]
        for d in (1, 2, 4):
            keep = row >= d
            U = U + A * jnp.where(keep, pltpu.roll(U, d, axis=0), 0.0)
            A = A * jnp.where(keep, pltpu.roll(A, d, axis=0), 1.0)
        H = A * h_prev + U
        hs.append(H)
        h_prev = jnp.broadcast_to(H[SUBLANES - 1:SUBLANES, :], (SUBLANES, W))
    h_ref[...] = h_prev
    h = jnp.concatenate(hs, axis=0)

    inner = gate * (GELU_K + (GELU_K * 0.044715) * (gate * gate))
    half = 0.5 * gate
    return (half + half * jnp.tanh(inner)) * h


def _in_proj_kernel(x_ref, wq_ref, wk_ref, wvt_ref, vt_one_ref, wf_ref, bf_ref, wxg_ref,
                    cw_ref, cb_ref, wri_ref, bri_ref, lam_ref,
                    qq_ref, kk_ref, vt_ref, rec_ref, carry_ref, ext_ref, h_ref):
    tm = x_ref.shape[1]

    @pl.when(pl.program_id(1) == 0)
    def _():
        carry_ref[...] = jnp.zeros_like(carry_ref)
        ext_ref[0:SUBLANES, :] = jnp.zeros((SUBLANES, LRU_WIDTH), F32)
        h_ref[...] = jnp.zeros_like(h_ref)

    xb = x_ref[0].astype(BF16)
    xg = jnp.dot(xb, wxg_ref[...], preferred_element_type=F32)
    rec = _lru_tile(xg[:, :LRU_WIDTH], xg[:, LRU_WIDTH:], cw_ref, cb_ref, wri_ref, bri_ref, lam_ref,
                    ext_ref, h_ref)
    rec_ref[0] = rec.astype(rec_ref.dtype)

    f = jnp.dot(xb, wf_ref[...], preferred_element_type=F32) + bf_ref[...]

    c = jnp.minimum(f, 0.0) - jnp.log1p(jnp.exp(-jnp.abs(f)))
    row = lax.broadcasted_iota(jnp.int32, c.shape, 0)
    d = 1
    while d < tm:
        c = c + jnp.where(row >= d, pltpu.roll(c, d, axis=0), 0.0)
        d *= 2
    c = c + carry_ref[...]
    carry_ref[...] = c[tm - 1:tm, :]

    c2 = c * LOG2E
    parts = []
    for h in range(N_HEADS):
        col = c2[:, h:h + 1]
        hi = col.astype(BF16).astype(F32)
        rem = col - hi
        mid = rem.astype(BF16).astype(F32)
        parts.append((hi, mid, rem - mid))

    lane = lax.broadcasted_iota(jnp.int32, (tm, LANES), 1)
    slot = lane % BIAS_SLOT
    head_slot = lane // BIAS_SLOT
    valid = head_slot < HEADS_PER_BLOCK
    for hb in range(N_HEAD_BLOCKS):
        def part(k):
            return jnp.where(head_slot == 0, parts[2 * hb][k], parts[2 * hb + 1][k])

        cq = jnp.where(slot == 0, part(0), jnp.where(slot == 1, part(1), part(2)))
        ck = jnp.where(slot == 3, part(0), jnp.where(slot == 4, part(1), part(2)))
        qx = jnp.where(valid & (slot < 3), cq, jnp.where(valid & (slot < 6), 1.0, 0.0))
        kx = jnp.where(valid & (slot < 3), 1.0, jnp.where(valid & (slot < 6), -ck, 0.0))
        qq_ref[0, :, (2 * hb + 1) * LANES:(2 * hb + 2) * LANES] = qx.astype(BF16)
        kk_ref[0, :, (2 * hb + 1) * LANES:(2 * hb + 2) * LANES] = kx.astype(BF16)

    vt = lax.dot_general(wvt_ref[...], xb, (((1,), (1,)), ((), ())), preferred_element_type=F32)
    vt_ref[0, 0] = (vt + vt_one_ref[...]).astype(BF16)

    q = jnp.dot(xb, wq_ref[...], preferred_element_type=F32) * (HEAD_DIM ** -0.5 * LOG2E)
    k = jnp.dot(xb, wk_ref[...], preferred_element_type=F32)
    for hb in range(N_HEAD_BLOCKS):
        qq_ref[0, :, 2 * hb * LANES:(2 * hb + 1) * LANES] = q[:, hb * LANES:(hb + 1) * LANES].astype(BF16)
        kk_ref[0, :, 2 * hb * LANES:(2 * hb + 1) * LANES] = k[:, hb * LANES:(hb + 1) * LANES].astype(BF16)


def _in_proj(x, wq, wk, wvt, vt_one, wf, bf, wxg, cw, cb, wri, bri, lam, tm):
    B, S, D = x.shape
    const = lambda b, s: (0, 0)
    tile = lambda b, s: (b, s, 0)
    weights = (wq, wk, wvt, vt_one, wf, bf, wxg, cw, cb, wri, bri, lam)
    return pl.pallas_call(
        _in_proj_kernel,
        grid=(B, S // tm),
        in_specs=[pl.BlockSpec((1, tm, D), tile)] + [pl.BlockSpec(w.shape, const) for w in weights],
        out_specs=[
            pl.BlockSpec((1, tm, 2 * ATT_WIDTH), tile),
            pl.BlockSpec((1, tm, 2 * ATT_WIDTH), tile),
            pl.BlockSpec((1, 1, N_HEADS * VT_ROWS, tm), lambda b, s: (b, s, 0, 0)),
            pl.BlockSpec((1, tm, LRU_WIDTH), tile),
        ],
        out_shape=[
            jax.ShapeDtypeStruct((B, S, 2 * ATT_WIDTH), BF16),
            jax.ShapeDtypeStruct((B, S, 2 * ATT_WIDTH), BF16),
            jax.ShapeDtypeStruct((B, S // tm, N_HEADS * VT_ROWS, tm), BF16),
            jax.ShapeDtypeStruct((B, S, LRU_WIDTH), BF16),
        ],
        scratch_shapes=[
            pltpu.VMEM((1, LANES), F32),
            pltpu.VMEM((tm + SUBLANES, LRU_WIDTH), F32),
            pltpu.VMEM((SUBLANES, LRU_WIDTH), F32),
        ],
        compiler_params=_cparams(("parallel", "arbitrary")),
        name="in_proj_lru",
    )(x, *weights)


def _attn_kernel(qq_ref, kk_ref, vt_ref, o_ref, st0_ref, st1_ref, bm0_ref, bm1_ref, m_ref, acc_ref, *, blk):
    qi = pl.program_id(2)
    st_refs = (st0_ref, st1_ref)
    bm_refs = (bm0_ref, bm1_ref)
    q = qq_ref[0]
    lane = lax.broadcasted_iota(jnp.int32, q.shape, 1)
    zero = jnp.zeros_like(q)
    q_heads = []
    for h in range(HEADS_PER_BLOCK):
        own = ((lane >= h * HEAD_DIM) & (lane < (h + 1) * HEAD_DIM)) | (
            (lane >= LANES + h * BIAS_SLOT) & (lane < LANES + (h + 1) * BIAS_SLOT))
        q_heads.append(jnp.where(own, q, zero))

    m_ref[...] = jnp.full_like(m_ref, NEG_INF)
    acc_ref[...] = jnp.zeros_like(acc_ref)

    def scores(ki, h, slot, masked):
        k_blk = kk_ref[0, pl.ds(pl.multiple_of(ki * blk, blk), blk), :]
        st = lax.dot_general(k_blk, q_heads[h], (((1,), (1,)), ((), ())),
                             preferred_element_type=F32)
        if masked:
            key = lax.broadcasted_iota(jnp.int32, (blk, blk), 0)
            qry = lax.broadcasted_iota(jnp.int32, (blk, blk), 1)
            st = jnp.where(key - qry <= (qi - ki) * blk, st, NEG_INF)
        st_refs[slot][h] = st
        bm_refs[slot][h] = jnp.max(st, axis=0, keepdims=True)

    def update(ki, h, slot):
        m_old = m_ref[h]
        m_new = jnp.maximum(m_old, bm_refs[slot][h])
        alpha = jnp.exp2(m_old - m_new)
        p = jnp.exp2(st_refs[slot][h] - m_new).astype(BF16)
        vt = vt_ref[0, ki, h * VT_ROWS:(h + 1) * VT_ROWS, :]
        acc_ref[h] = alpha * acc_ref[h] + jnp.dot(vt, p, preferred_element_type=F32)
        m_ref[h] = m_new

    def stage(score_blk, score_slot, masked, update_blk, update_slot):
        for h in range(HEADS_PER_BLOCK):
            if score_blk is not None:
                scores(score_blk, h, score_slot, masked)
            if update_blk is not None:
                update(update_blk, h, update_slot)

    stage(0, 0, True, None, None)

    def body(j, carry):
        i = 2 * j
        stage(i + 1, 1, False, i, 0)
        stage(i + 2, 0, False, i + 1, 1)
        return carry

    lax.fori_loop(0, (qi - 1) // 2, body, 0)

    @pl.when(qi == 0)
    def _():
        stage(None, None, False, 0, 0)

    @pl.when(qi % 2 == 1)
    def _():
        stage(qi, 1, True, qi - 1, 0)
        stage(None, None, False, qi, 1)

    @pl.when((qi % 2 == 0) & (qi > 0))
    def _():
        stage(qi - 1, 1, False, qi - 2, 0)
        stage(qi, 0, True, qi - 1, 1)
        stage(None, None, False, qi, 0)

    ot = jnp.concatenate([acc_ref[h, :HEAD_DIM, :] / acc_ref[h, HEAD_DIM:HEAD_DIM + 1, :]
                          for h in range(HEADS_PER_BLOCK)], axis=0)
    o_ref[0] = ot.T.astype(o_ref.dtype)


def _attention(qq, kk, vt, blk):
    B, nblk, _, _ = vt.shape
    S = nblk * blk
    return pl.pallas_call(
        functools.partial(_attn_kernel, blk=blk),
        grid=(B, N_HEAD_BLOCKS, nblk),
        in_specs=[
            pl.BlockSpec((1, blk, 2 * LANES), lambda b, h, i: (b, i, h)),
            pl.BlockSpec((1, S, 2 * LANES), lambda b, h, i: (b, 0, h)),
            pl.BlockSpec((1, nblk, HEADS_PER_BLOCK * VT_ROWS, blk), lambda b, h, i: (b, 0, h, 0)),
        ],
        out_specs=pl.BlockSpec((1, blk, LANES), lambda b, h, i: (b, i, h)),
        out_shape=jax.ShapeDtypeStruct((B, S, ATT_WIDTH), BF16),
        scratch_shapes=[
            pltpu.VMEM((HEADS_PER_BLOCK, blk, blk), F32),
            pltpu.VMEM((HEADS_PER_BLOCK, blk, blk), F32),
            pltpu.VMEM((HEADS_PER_BLOCK, 1, blk), F32),
            pltpu.VMEM((HEADS_PER_BLOCK, 1, blk), F32),
            pltpu.VMEM((HEADS_PER_BLOCK, 1, blk), F32),
            pltpu.VMEM((HEADS_PER_BLOCK, VT_ROWS, blk), F32),
        ],
        compiler_params=_cparams(("parallel", "parallel", "arbitrary")),
        name="fox_attention",
    )(qq, kk, vt)


def _layer_norm(y, g, b):
    mu = jnp.mean(y, axis=-1, keepdims=True)
    yc = y - mu
    var = jnp.mean(yc * yc, axis=-1, keepdims=True)
    return yc * lax.rsqrt(var + LN_EPS) * g + b


def _route(logits_t):
    mx = jnp.max(logits_t, axis=0, keepdims=True)
    e = jnp.exp(logits_t - mx)
    p = e / jnp.sum(e, axis=0, keepdims=True)
    v = [p[SUBLANES * j:SUBLANES * (j + 1), :] for j in range(EXPERTS_PER_GROUP)]

    def first_max(vals):
        best = functools.reduce(jnp.maximum, vals)
        idx = jnp.full(best.shape, len(vals) - 1, jnp.int32)
        for j in range(len(vals) - 2, -1, -1):
            idx = jnp.where(vals[j] == best, j, idx)
        return best, idx

    m1, idx1 = first_max(v)
    m2, idx2 = first_max([jnp.where(idx1 == j, -1.0, v[j]) for j in range(EXPERTS_PER_GROUP)])
    grp = lax.broadcasted_iota(jnp.int32, m1.shape, 0)
    real = grp < N_GROUPS
    score = jnp.where(real, m1 + m2, -1.0)
    best = jnp.max(score, axis=0, keepdims=True)
    chosen = jnp.min(jnp.where(score == best, grp, SUBLANES), axis=0, keepdims=True)
    sel = grp == chosen
    den = jnp.where(real, m1 + m2, 1.0)
    g1 = m1 / den
    g2 = m2 / den
    comb = [jnp.where(sel, jnp.where(idx1 == j, g1, 0.0) + jnp.where(idx2 == j, g2, 0.0), 0.0)
            for j in range(EXPERTS_PER_GROUP)]
    return comb, chosen


def _sort_by_group(chosen, tri_ref):
    tm = chosen.shape[1]
    grp = lax.broadcasted_iota(jnp.int32, (SUBLANES, tm), 0)
    member = grp == chosen
    onehot = jnp.where(member, 1.0, 0.0).astype(BF16)
    count = jnp.dot(onehot, tri_ref[...], preferred_element_type=F32)
    total = count[:, tm - 1:tm]
    gcol = lax.broadcasted_iota(jnp.int32, (SUBLANES, 1), 0)
    start = jnp.zeros((SUBLANES, 1), F32)
    for k in range(N_GROUPS - 1):
        start = start + jnp.where(gcol > k, total[k:k + 1, :], 0.0)
    pos = jnp.sum(jnp.where(member, start + count - 1.0, 0.0), axis=0, keepdims=True)
    return pos, start, start + total


def _mix_kernel(x_ref, attn_ref, rec_ref, ng_ref, wout_ref, lng_ref, lnb_ref, wr_ref, br_ref, tri_ref,
                x1_ref, x1b_ref, comb_ref, pos_ref, seg_ref, *, alpha):
    tm = x_ref.shape[0]
    chunk = min(MIX_CHUNK, tm)
    rows = ROUTE_ROWS
    nt = (((1,), (1,)), ((), ()))
    comb_parts, chosen_parts = [], []
    for r0 in range(0, tm, chunk):
        rs = slice(r0, r0 + chunk)
        ya = attn_ref[rs, :].astype(F32)
        yr = rec_ref[rs, :].astype(F32)
        ya = ya * lax.rsqrt(jnp.mean(ya * ya, axis=-1, keepdims=True) + RMS_EPS) * ng_ref[:, :ATT_WIDTH]
        yr = yr * lax.rsqrt(jnp.mean(yr * yr, axis=-1, keepdims=True) + RMS_EPS) * ng_ref[:, ATT_WIDTH:]
        out = jnp.dot(ya.astype(BF16), wout_ref[:ATT_WIDTH, :], preferred_element_type=F32)
        out = out + jnp.dot(yr.astype(BF16), wout_ref[ATT_WIDTH:, :], preferred_element_type=F32)
        x1 = _layer_norm(alpha * x_ref[rs, :] + out, lng_ref[...], lnb_ref[...])
        x1_ref[rs, :] = x1
        x1_hi = x1.astype(BF16)
        x1b_ref[rs, :] = x1_hi

        x1_lo = (x1 - x1_hi.astype(F32)).astype(BF16)
        o = (lax.dot_general(wr_ref[...], x1_hi, nt, preferred_element_type=F32)
             + lax.dot_general(wr_ref[...], x1_lo, nt, preferred_element_type=F32))
        logits_t = o[:rows, :] + o[rows:, :] + br_ref[...]
        comb_t, chosen = _route(logits_t)
        comb_parts.append(jnp.concatenate(comb_t, axis=0))
        chosen_parts.append(chosen)

    chosen = jnp.concatenate(chosen_parts, axis=1)
    pos, seg_start, seg_end = _sort_by_group(chosen, tri_ref)
    pos_ref[0] = pos
    lane = lax.broadcasted_iota(jnp.int32, (SUBLANES, LANES), 1)
    seg_ref[0] = jnp.where(lane == 0, seg_start, jnp.where(lane == 1, seg_end, 0.0))
    meta_row = lax.broadcasted_iota(jnp.int32, (SUBLANES, tm), 0)
    meta = jnp.where(meta_row == 0, chosen.astype(F32), jnp.where(meta_row == 1, pos, 0.0))
    comb_t = jnp.concatenate([jnp.concatenate(comb_parts, axis=1), meta,
                              jnp.zeros((LANES - rows - SUBLANES, tm), F32)], axis=0)
    comb_ref[...] = comb_t.T


def _mix_out(x2d, attn2d, rec2d, ng, wout, lng, lnb, wr, br, tri, alpha, tm):
    T, D = x2d.shape
    const = lambda i: (0, 0)
    tile = lambda i: (i, 0)
    n_tiles = T // tm
    return pl.pallas_call(
        functools.partial(_mix_kernel, alpha=alpha),
        grid=(T // tm,),
        in_specs=[
            pl.BlockSpec((tm, D), tile),
            pl.BlockSpec((tm, ATT_WIDTH), tile),
            pl.BlockSpec((tm, LRU_WIDTH), tile),
            pl.BlockSpec(ng.shape, const),
            pl.BlockSpec(wout.shape, const),
            pl.BlockSpec(lng.shape, const),
            pl.BlockSpec(lnb.shape, const),
            pl.BlockSpec(wr.shape, const),
            pl.BlockSpec(br.shape, const),
            pl.BlockSpec(tri.shape, const),
        ],
        out_specs=[
            pl.BlockSpec((tm, D), tile),
            pl.BlockSpec((tm, D), tile),
            pl.BlockSpec((tm, LANES), tile),
            pl.BlockSpec((1, 1, tm), lambda i: (i, 0, 0)),
            pl.BlockSpec((1, SUBLANES, LANES), lambda i: (i, 0, 0)),
        ],
        out_shape=[
            jax.ShapeDtypeStruct((T, D), F32),
            jax.ShapeDtypeStruct((T, D), BF16),
            jax.ShapeDtypeStruct((T, LANES), F32),
            jax.ShapeDtypeStruct((n_tiles, 1, tm), F32),
            jax.ShapeDtypeStruct((n_tiles, SUBLANES, LANES), F32),
        ],
        compiler_params=_cparams(("parallel",)),
        name="mix_out_router",
    )(x2d, attn2d, rec2d, ng, wout, lng, lnb, wr, br, tri)


def _moe_kernel(seg_ref, x1_ref, xb_ref, comb_ref, pos_ref, wgu_ref, wd_ref, lng_ref, lnb_ref, o_ref,
                xs_ref, cs_ref, ys_ref, *, alpha):
    i = pl.program_id(0)
    g = pl.program_id(1)
    tm = xb_ref.shape[0]
    F = wd_ref.shape[1]

    @pl.when(g == 0)
    def _():
        pos = pos_ref[0].astype(jnp.int32)
        comb = comb_ref[...]
        hi = comb.astype(BF16)
        lo = (comb - hi.astype(F32)).astype(BF16)
        comb_hl = jnp.concatenate([hi, lo], axis=1)
        for r0 in range(0, tm, PERM_ROWS):
            srow = r0 + lax.broadcasted_iota(jnp.int32, (PERM_ROWS, tm), 0)
            perm = jnp.where(srow == pos, 1.0, 0.0).astype(BF16)
            xs_ref[r0:r0 + PERM_ROWS, :] = jnp.dot(perm, xb_ref[...], preferred_element_type=F32).astype(BF16)
            c2 = jnp.dot(perm, comb_hl, preferred_element_type=F32)
            cs_ref[r0:r0 + PERM_ROWS, :] = c2[:, :LANES] + c2[:, LANES:]
        ys_ref[...] = jnp.zeros_like(ys_ref)

    start = seg_ref[(i * N_GROUPS + g) * 2]
    end = seg_ref[(i * N_GROUPS + g) * 2 + 1]
    base = (start // BF16_SUBLANES) * BF16_SUBLANES
    n_chunks = (end - base + MOE_CHUNK - 1) // MOE_CHUNK
    lane = lax.broadcasted_iota(jnp.int32, (MOE_CHUNK, LANES), 1)
    crow = lax.broadcasted_iota(jnp.int32, (MOE_CHUNK, 1), 0)

    def chunk(c, carry):
        lo_row = base + c * MOE_CHUNK
        r0 = pl.multiple_of(jnp.minimum(lo_row, tm - MOE_CHUNK), BF16_SUBLANES)
        rows = pl.ds(r0, MOE_CHUNK)
        xs = xs_ref[rows, :]
        cs = cs_ref[rows, :]
        grow = r0 + crow
        live = (grow >= jnp.maximum(start, lo_row)) & (grow < jnp.minimum(end, lo_row + MOE_CHUNK))
        acc = ys_ref[rows, :]
        for j in range(EXPERTS_PER_GROUP):
            gu = jnp.dot(xs, wgu_ref[j], preferred_element_type=F32)
            gate = gu[:, :F]
            w = jnp.sum(jnp.where(lane == SUBLANES * j + g, cs, 0.0), axis=1, keepdims=True)
            w = jnp.where(live, w, 0.0)
            act = gate * jax.nn.sigmoid(gate) * gu[:, F:] * w
            acc = acc + jnp.dot(act.astype(BF16), wd_ref[j], preferred_element_type=F32)
        ys_ref[rows, :] = acc
        return carry

    lax.fori_loop(0, n_chunks, chunk, 0)

    @pl.when(g == N_GROUPS - 1)
    def _():
        ys = ys_ref[...].astype(BF16)
        scol = lax.broadcasted_iota(jnp.int32, (PERM_ROWS, tm), 1)
        for r0 in range(0, tm, PERM_ROWS):
            rs = slice(r0, r0 + PERM_ROWS)
            pos_col = comb_ref[rs, ROUTE_ROWS + 1:ROUTE_ROWS + 2].astype(jnp.int32)
            perm = jnp.where(scol == pos_col, 1.0, 0.0).astype(BF16)
            y = jnp.dot(perm, ys, preferred_element_type=F32)
            o_ref[rs, :] = _layer_norm(alpha * x1_ref[rs, :] + y, lng_ref[...], lnb_ref[...])


def _moe(seg, x1, x1b, comb, pos, wgu, wd, lng, lnb, alpha, tm):
    T, D = x1.shape
    F = wd.shape[1]
    const = lambda i, g, seg: (0, 0)
    tile = lambda i, g, seg: (i, 0)
    return pl.pallas_call(
        functools.partial(_moe_kernel, alpha=alpha),
        grid_spec=pltpu.PrefetchScalarGridSpec(
            num_scalar_prefetch=1,
            grid=(T // tm, N_GROUPS),
            in_specs=[
                pl.BlockSpec((tm, D), tile),
                pl.BlockSpec((tm, D), tile),
                pl.BlockSpec((tm, LANES), tile),
                pl.BlockSpec((1, 1, tm), lambda i, g, seg: (i, 0, 0)),
                pl.BlockSpec((EXPERTS_PER_GROUP, D, 2 * F), lambda i, g, seg: (g, 0, 0)),
                pl.BlockSpec((EXPERTS_PER_GROUP, F, D), lambda i, g, seg: (g, 0, 0)),
                pl.BlockSpec(lng.shape, const),
                pl.BlockSpec(lnb.shape, const),
            ],
            out_specs=pl.BlockSpec((tm, D), tile),
            scratch_shapes=[
                pltpu.VMEM((tm, D), BF16),
                pltpu.VMEM((tm, LANES), F32),
                pltpu.VMEM((tm, D), F32),
            ],
        ),
        out_shape=jax.ShapeDtypeStruct((T, D), F32),
        compiler_params=_cparams(("parallel", "arbitrary")),
        name="moe_ffn",
    )(seg, x1, x1b, comb, pos, wgu, wd, lng, lnb)


def _block_diag(w):
    n, c, d = w.shape
    return jnp.einsum('ncd,nm->ncmd', w, jnp.eye(n, dtype=w.dtype)).reshape(n * c, n * d)


def _router_operands(w_router, b_router):
    D = w_router.shape[0]
    wt = w_router.T.reshape(N_GROUPS, EXPERTS_PER_GROUP, D).transpose(1, 0, 2)
    wt = jnp.pad(wt, ((0, 0), (0, SUBLANES - N_GROUPS), (0, 0))).reshape(4 * SUBLANES, D)
    hi = wt.astype(BF16)
    lo = (wt - hi.astype(F32)).astype(BF16)
    bt = b_router.reshape(N_GROUPS, EXPERTS_PER_GROUP).T
    bt = jnp.pad(bt, ((0, 0), (0, SUBLANES - N_GROUPS)), constant_values=NEG_INF).reshape(4 * SUBLANES, 1)
    return jnp.concatenate([hi, lo], axis=0), bt.astype(F32)


def kernel(x, w_in, b_forget, conv_w, conv_b, w_rec_gate, b_rec_gate, w_in_gate, b_in_gate, lru_lambda,
           mix_norm_g, w_out, ln1_g, ln1_b, w_router, b_router, w_exp_gate, w_exp_up, w_exp_down,
           ln2_g, ln2_b):
    B, S, D = x.shape
    depth = w_in.shape[0]
    alpha = (2.0 * depth) ** 0.25
    T = B * S
    o_f = 3 * ATT_WIDTH
    o_x = o_f + N_HEADS

    blk_attn = min(512, S)
    tm_moe = min(1024, T)
    tri = jnp.triu(jnp.ones((tm_moe, tm_moe), F32)).astype(BF16)

    wr, br = _router_operands(w_router, b_router)
    row = lambda a: a.reshape(1, -1).astype(F32)
    vt_one = (jnp.arange(N_HEADS * VT_ROWS) % VT_ROWS == HEAD_DIM).astype(F32).reshape(-1, 1)

    for l in range(depth):
        wq = w_in[l, :, :ATT_WIDTH].astype(BF16)
        wk = w_in[l, :, ATT_WIDTH:2 * ATT_WIDTH].astype(BF16)
        wvt = w_in[l, :, 2 * ATT_WIDTH:o_f].T.reshape(N_HEADS, HEAD_DIM, D)
        wvt = jnp.pad(wvt, ((0, 0), (0, VT_ROWS - HEAD_DIM), (0, 0))).reshape(N_HEADS * VT_ROWS, D).astype(BF16)
        wf = jnp.pad(w_in[l, :, o_f:o_x], ((0, 0), (0, LANES - N_HEADS))).astype(BF16)
        bf = jnp.pad(b_forget[l], (0, LANES - N_HEADS)).reshape(1, LANES).astype(F32)
        wxg = w_in[l, :, o_x:].astype(BF16)
        wri = jnp.concatenate([_block_diag(w_rec_gate[l]), _block_diag(w_in_gate[l])], axis=1).astype(BF16)
        bri = jnp.concatenate([b_rec_gate[l], b_in_gate[l]]).reshape(1, -1).astype(F32)
        qq, kk, vt, rec = _in_proj(x, wq, wk, wvt, vt_one, wf, bf, wxg, conv_w[l].astype(F32), row(conv_b[l]),
                                   wri, bri, row(lru_lambda[l]), blk_attn)

        attn = _attention(qq, kk, vt, blk_attn)

        x1, x1b, comb, pos, seg = _mix_out(
            x.reshape(T, D), attn.reshape(T, ATT_WIDTH), rec.reshape(T, LRU_WIDTH), row(mix_norm_g[l]),
            w_out[l].astype(BF16), row(ln1_g[l]), row(ln1_b[l]), wr, br, tri, alpha, tm_moe)

        seg = seg[:, :N_GROUPS, :2].astype(jnp.int32).reshape(-1)
        wgu = jnp.concatenate([w_exp_gate[l], w_exp_up[l]], axis=-1).astype(BF16)
        x = _moe(seg, x1, x1b, comb, pos, wgu, w_exp_down[l].astype(BF16), row(ln2_g[l]), row(ln2_b[l]),
                 alpha, tm_moe).reshape(B, S, D).astype(x.dtype)
    return x
```

```python
import functools

import jax
import jax.numpy as jnp
from jax import lax
from jax.experimental import pallas as pl
from jax.experimental.pallas import tpu as pltpu

F32 = jnp.float32
BF16 = jnp.bfloat16

HEAD_DIM = 64
N_HEADS = 8
ATT_WIDTH = N_HEADS * HEAD_DIM
LRU_WIDTH = 512
LRU_C = 8.0
CONV_W = 4
N_GROUPS = 4
EXPERTS_PER_GROUP = 4
N_EXPERTS = N_GROUPS * EXPERTS_PER_GROUP
LN_EPS = 1e-5
RMS_EPS = 1e-6
NEG_INF = -1e30

LANES = 128
SUBLANES = 8
HEADS_PER_BLOCK = LANES // HEAD_DIM
N_HEAD_BLOCKS = N_HEADS // HEADS_PER_BLOCK
BIAS_SLOT = 8
BF16_SUBLANES = 16
VT_ROWS = HEAD_DIM + BF16_SUBLANES
LOG2E = 1.4426950408889634
GELU_K = 0.7978845608028654
VMEM_LIMIT = 48 * 1024 * 1024
MIX_CHUNK = 256
SORT_TILE = 512
MOE_CHUNK = 160
PERM_ROWS = 256
ROUTE_ROWS = 4 * SUBLANES


def _cparams(sem):
    return pltpu.CompilerParams(dimension_semantics=sem, vmem_limit_bytes=VMEM_LIMIT)


def _lru_tile(x, gate, cw_ref, cb_ref, wri_ref, bri_ref, lam_ref, ext_ref, h_ref):
    ts, W = x.shape
    ext_ref[SUBLANES:SUBLANES + ts, :] = x
    xc = cb_ref[...] + cw_ref[CONV_W - 1:CONV_W, :] * x
    for back in range(1, CONV_W):
        xc = xc + cw_ref[CONV_W - 1 - back:CONV_W - back, :] * ext_ref[SUBLANES - back:SUBLANES - back + ts, :]
    ext_ref[0:SUBLANES, :] = x[ts - SUBLANES:ts, :]

    ri = jnp.dot(xc.astype(BF16), wri_ref[...], preferred_element_type=F32) + bri_ref[...]
    r = 0.5 * jnp.tanh(0.5 * ri[:, :W]) + 0.5
    i = 0.5 * jnp.tanh(0.5 * ri[:, W:]) + 0.5
    neg_lam = -lam_ref[...]
    softplus = jnp.maximum(neg_lam, 0.0) + jnp.log1p(jnp.exp(-jnp.abs(neg_lam)))
    a = jnp.exp(-LRU_C * r * softplus)
    v = jnp.maximum(1.0 - a * a, 0.0)
    u = jnp.where(v > 0.0, v * lax.rsqrt(v), 0.0) * (i * xc)

    row = lax.broadcasted_iota(jnp.int32, (SUBLANES, W), 0)
    h_prev = h_ref[...]
    hs = []
    for j in range(ts // SUBLANES):
        A = a[j * SUBLANES:(j + 1) * SUBLANES, :]
        U = u[j * SUBLANES:(j + 1) * SUBLANES, :]
        for d in (1, 2, 4):
            keep = row >= d
            U = U + A * jnp.where(keep, pltpu.roll(U, d, axis=0), 0.0)
            A = A * jnp.where(keep, pltpu.roll(A, d, axis=0), 1.0)
        H = A * h_prev + U
        hs.append(H)
        h_prev = jnp.broadcast_to(H[SUBLANES - 1:SUBLANES, :], (SUBLANES, W))
    h_ref[...] = h_prev
    h = jnp.concatenate(hs, axis=0)

    inner = gate * (GELU_K + (GELU_K * 0.044715) * (gate * gate))
    half = 0.5 * gate
    return (half + half * jnp.tanh(inner)) * h


def _in_proj_kernel(x_ref, wq_ref, wk_ref, wvt_ref, vt_one_ref, wf_ref, bf_ref, wxg_ref,
                    cw_ref, cb_ref, wri_ref, bri_ref, lam_ref,
                    qq_ref, kk_ref, vt_ref, rec_ref, carry_ref, ext_ref, h_ref):
    tm = x_ref.shape[1]

    @pl.when(pl.program_id(1) == 0)
    def _():
        carry_ref[...] = jnp.zeros_like(carry_ref)
        ext_ref[0:SUBLANES, :] = jnp.zeros((SUBLANES, LRU_WIDTH), F32)
        h_ref[...] = jnp.zeros_like(h_ref)

    xb = x_ref[0].astype(BF16)
    xg = jnp.dot(xb, wxg_ref[...], preferred_element_type=F32)
    rec = _lru_tile(xg[:, :LRU_WIDTH], xg[:, LRU_WIDTH:], cw_ref, cb_ref, wri_ref, bri_ref, lam_ref,
                    ext_ref, h_ref)
    rec_ref[0] = rec.astype(rec_ref.dtype)

    f = jnp.dot(xb, wf_ref[...], preferred_element_type=F32) + bf_ref[...]

    c = jnp.minimum(f, 0.0) - jnp.log1p(jnp.exp(-jnp.abs(f)))
    row = lax.broadcasted_iota(jnp.int32, c.shape, 0)
    d = 1
    while d < tm:
        c = c + jnp.where(row >= d, pltpu.roll(c, d, axis=0), 0.0)
        d *= 2
    c = c + carry_ref[...]
    carry_ref[...] = c[tm - 1:tm, :]

    c2 = c * LOG2E
    parts = []
    for h in range(N_HEADS):
        col = c2[:, h:h + 1]
        hi = col.astype(BF16).astype(F32)
        rem = col - hi
        mid = rem.astype(BF16).astype(F32)
        parts.append((hi, mid, rem - mid))

    lane = lax.broadcasted_iota(jnp.int32, (tm, LANES), 1)
    slot = lane % BIAS_SLOT
    head_slot = lane // BIAS_SLOT
    valid = head_slot < HEADS_PER_BLOCK
    for hb in range(N_HEAD_BLOCKS):
        def part(k):
            return jnp.where(head_slot == 0, parts[2 * hb][k], parts[2 * hb + 1][k])

        cq = jnp.where(slot == 0, part(0), jnp.where(slot == 1, part(1), part(2)))
        ck = jnp.where(slot == 3, part(0), jnp.where(slot == 4, part(1), part(2)))
        qx = jnp.where(valid & (slot < 3), cq, jnp.where(valid & (slot < 6), 1.0, 0.0))
        kx = jnp.where(valid & (slot < 3), 1.0, jnp.where(valid & (slot < 6), -ck, 0.0))
        qq_ref[0, :, (2 * hb + 1) * LANES:(2 * hb + 2) * LANES] = qx.astype(BF16)
        kk_ref[0, :, (2 * hb + 1) * LANES:(2 * hb + 2) * LANES] = kx.astype(BF16)

    vt = lax.dot_general(wvt_ref[...], xb, (((1,), (1,)), ((), ())), preferred_element_type=F32)
    vt_ref[0, 0] = (vt + vt_one_ref[...]).astype(BF16)

    q = jnp.dot(xb, wq_ref[...], preferred_element_type=F32) * (HEAD_DIM ** -0.5 * LOG2E)
    k = jnp.dot(xb, wk_ref[...], preferred_element_type=F32)
    for hb in range(N_HEAD_BLOCKS):
        qq_ref[0, :, 2 * hb * LANES:(2 * hb + 1) * LANES] = q[:, hb * LANES:(hb + 1) * LANES].astype(BF16)
        kk_ref[0, :, 2 * hb * LANES:(2 * hb + 1) * LANES] = k[:, hb * LANES:(hb + 1) * LANES].astype(BF16)


def _in_proj(x, wq, wk, wvt, vt_one, wf, bf, wxg, cw, cb, wri, bri, lam, tm):
    B, S, D = x.shape
    const = lambda b, s: (0, 0)
    tile = lambda b, s: (b, s, 0)
    weights = (wq, wk, wvt, vt_one, wf, bf, wxg, cw, cb, wri, bri, lam)
    return pl.pallas_call(
        _in_proj_kernel,
        grid=(B, S // tm),
        in_specs=[pl.BlockSpec((1, tm, D), tile)] + [pl.BlockSpec(w.shape, const) for w in weights],
        out_specs=[
            pl.BlockSpec((1, tm, 2 * ATT_WIDTH), tile),
            pl.BlockSpec((1, tm, 2 * ATT_WIDTH), tile),
            pl.BlockSpec((1, 1, N_HEADS * VT_ROWS, tm), lambda b, s: (b, s, 0, 0)),
            pl.BlockSpec((1, tm, LRU_WIDTH), tile),
        ],
        out_shape=[
            jax.ShapeDtypeStruct((B, S, 2 * ATT_WIDTH), BF16),
            jax.ShapeDtypeStruct((B, S, 2 * ATT_WIDTH), BF16),
            jax.ShapeDtypeStruct((B, S // tm, N_HEADS * VT_ROWS, tm), BF16),
            jax.ShapeDtypeStruct((B, S, LRU_WIDTH), BF16),
        ],
        scratch_shapes=[
            pltpu.VMEM((1, LANES), F32),
            pltpu.VMEM((tm + SUBLANES, LRU_WIDTH), F32),
            pltpu.VMEM((SUBLANES, LRU_WIDTH), F32),
        ],
        compiler_params=_cparams(("parallel", "arbitrary")),
        name="in_proj_lru",
    )(x, *weights)


def _attn_kernel(qq_ref, kk_ref, vt_ref, o_ref, st0_ref, st1_ref, bm0_ref, bm1_ref, m_ref, acc_ref, *, blk):
    qi = pl.program_id(2)
    st_refs = (st0_ref, st1_ref)
    bm_refs = (bm0_ref, bm1_ref)
    q = qq_ref[0]
    lane = lax.broadcasted_iota(jnp.int32, q.shape, 1)
    zero = jnp.zeros_like(q)
    q_heads = []
    for h in range(HEADS_PER_BLOCK):
        own = ((lane >= h * HEAD_DIM) & (lane < (h + 1) * HEAD_DIM)) | (
            (lane >= LANES + h * BIAS_SLOT) & (lane < LANES + (h + 1) * BIAS_SLOT))
        q_heads.append(jnp.where(own, q, zero))

    m_ref[...] = jnp.full_like(m_ref, NEG_INF)
    acc_ref[...] = jnp.zeros_like(acc_ref)

    def scores(ki, h, slot, masked):
        k_blk = kk_ref[0, pl.ds(pl.multiple_of(ki * blk, blk), blk), :]
        st = lax.dot_general(k_blk, q_heads[h], (((1,), (1,)), ((), ())),
                             preferred_element_type=F32)
        if masked:
            key = lax.broadcasted_iota(jnp.int32, (blk, blk), 0)
            qry = lax.broadcasted_iota(jnp.int32, (blk, blk), 1)
            st = jnp.where(key - qry <= (qi - ki) * blk, st, NEG_INF)
        st_refs[slot][h] = st
        bm_refs[slot][h] = jnp.max(st, axis=0, keepdims=True)

    def update(ki, h, slot):
        m_old = m_ref[h]
        m_new = jnp.maximum(m_old, bm_refs[slot][h])
        alpha = jnp.exp2(m_old - m_new)
        p = jnp.exp2(st_refs[slot][h] - m_new).astype(BF16)
        vt = vt_ref[0, ki, h * VT_ROWS:(h + 1) * VT_ROWS, :]
        acc_ref[h] = alpha * acc_ref[h] + jnp.dot(vt, p, preferred_element_type=F32)
        m_ref[h] = m_new

    def stage(score_blk, score_slot, masked, update_blk, update_slot):
        for h in range(HEADS_PER_BLOCK):
            if score_blk is not None:
                scores(score_blk, h, score_slot, masked)
            if update_blk is not None:
                update(update_blk, h, update_slot)

    stage(0, 0, True, None, None)

    def body(j, carry):
        i = 2 * j
        stage(i + 1, 1, False, i, 0)
        stage(i + 2, 0, False, i + 1, 1)
        return carry

    lax.fori_loop(0, (qi - 1) // 2, body, 0)

    @pl.when(qi == 0)
    def _():
        stage(None, None, False, 0, 0)

    @pl.when(qi % 2 == 1)
    def _():
        stage(qi, 1, True, qi - 1, 0)
        stage(None, None, False, qi, 1)

    @pl.when((qi % 2 == 0) & (qi > 0))
    def _():
        stage(qi - 1, 1, False, qi - 2, 0)
        stage(qi, 0, True, qi - 1, 1)
        stage(None, None, False, qi, 0)

    ot = jnp.concatenate([acc_ref[h, :HEAD_DIM, :] / acc_ref[h, HEAD_DIM:HEAD_DIM + 1, :]
                          for h in range(HEADS_PER_BLOCK)], axis=0)
    o_ref[0] = ot.T.astype(o_ref.dtype)


def _attention(qq, kk, vt, blk):
    B, nblk, _, _ = vt.shape
    S = nblk * blk
    return pl.pallas_call(
        functools.partial(_attn_kernel, blk=blk),
        grid=(B, N_HEAD_BLOCKS, nblk),
        in_specs=[
            pl.BlockSpec((1, blk, 2 * LANES), lambda b, h, i: (b, i, h)),
            pl.BlockSpec((1, S, 2 * LANES), lambda b, h, i: (b, 0, h)),
            pl.BlockSpec((1, nblk, HEADS_PER_BLOCK * VT_ROWS, blk), lambda b, h, i: (b, 0, h, 0)),
        ],
        out_specs=pl.BlockSpec((1, blk, LANES), lambda b, h, i: (b, i, h)),
        out_shape=jax.ShapeDtypeStruct((B, S, ATT_WIDTH), BF16),
        scratch_shapes=[
            pltpu.VMEM((HEADS_PER_BLOCK, blk, blk), F32),
            pltpu.VMEM((HEADS_PER_BLOCK, blk, blk), F32),
            pltpu.VMEM((HEADS_PER_BLOCK, 1, blk), F32),
            pltpu.VMEM((HEADS_PER_BLOCK, 1, blk), F32),
            pltpu.VMEM((HEADS_PER_BLOCK, 1, blk), F32),
            pltpu.VMEM((HEADS_PER_BLOCK, VT_ROWS, blk), F32),
        ],
        compiler_params=_cparams(("parallel", "parallel", "arbitrary")),
        name="fox_attention",
    )(qq, kk, vt)


def _layer_norm(y, g, b):
    mu = jnp.mean(y, axis=-1, keepdims=True)
    yc = y - mu
    var = jnp.mean(yc * yc, axis=-1, keepdims=True)
    return yc * lax.rsqrt(var + LN_EPS) * g + b


def _route(logits_t):
    mx = jnp.max(logits_t, axis=0, keepdims=True)
    e = jnp.exp(logits_t - mx)
    p = e / jnp.sum(e, axis=0, keepdims=True)
    v = [p[SUBLANES * j:SUBLANES * (j + 1), :] for j in range(EXPERTS_PER_GROUP)]

    def first_max(vals):
        best = functools.reduce(jnp.maximum, vals)
        idx = jnp.full(best.shape, len(vals) - 1, jnp.int32)
        for j in range(len(vals) - 2, -1, -1):
            idx = jnp.where(vals[j] == best, j, idx)
        return best, idx

    m1, idx1 = first_max(v)
    m2, idx2 = first_max([jnp.where(idx1 == j, -1.0, v[j]) for j in range(EXPERTS_PER_GROUP)])
    grp = lax.broadcasted_iota(jnp.int32, m1.shape, 0)
    real = grp < N_GROUPS
    score = jnp.where(real, m1 + m2, -1.0)
    best = jnp.max(score, axis=0, keepdims=True)
    chosen = jnp.min(jnp.where(score == best, grp, SUBLANES), axis=0, keepdims=True)
    sel = grp == chosen
    den = jnp.where(real, m1 + m2, 1.0)
    g1 = m1 / den
    g2 = m2 / den
    comb = [jnp.where(sel, jnp.where(idx1 == j, g1, 0.0) + jnp.where(idx2 == j, g2, 0.0), 0.0)
            for j in range(EXPERTS_PER_GROUP)]
    return comb, chosen


def _sort_by_group(chosen, tri_ref):
    tm = chosen.shape[1]
    grp = lax.broadcasted_iota(jnp.int32, (SUBLANES, tm), 0)
    member = grp == chosen
    onehot = jnp.where(member, 1.0, 0.0).astype(BF16)
    count = jnp.dot(onehot, tri_ref[...], preferred_element_type=F32)
    total = count[:, tm - 1:tm]
    gcol = lax.broadcasted_iota(jnp.int32, (SUBLANES, 1), 0)
    start = jnp.zeros((SUBLANES, 1), F32)
    for k in range(N_GROUPS - 1):
        start = start + jnp.where(gcol > k, total[k:k + 1, :], 0.0)
    pos = jnp.sum(jnp.where(member, start + count - 1.0, 0.0), axis=0, keepdims=True)
    return pos, start, start + total


def _mix_kernel(x_ref, attn_ref, rec_ref, ng_ref, wout_ref, lng_ref, lnb_ref, wr_ref, br_ref, tri_ref,
                x1_ref, x1b_ref, comb_ref, pos_ref, seg_ref, *, alpha):
    tm = x_ref.shape[0]
    chunk = min(MIX_CHUNK, tm)
    rows = ROUTE_ROWS
    nt = (((1,), (1,)), ((), ()))
    comb_parts, chosen_parts = [], []

    def project(r0):
        rs = slice(r0, r0 + chunk)
        ya = attn_ref[rs, :].astype(F32)
        yr = rec_ref[rs, :].astype(F32)
        ya = ya * lax.rsqrt(jnp.mean(ya * ya, axis=-1, keepdims=True) + RMS_EPS) * ng_ref[:, :ATT_WIDTH]
        yr = yr * lax.rsqrt(jnp.mean(yr * yr, axis=-1, keepdims=True) + RMS_EPS) * ng_ref[:, ATT_WIDTH:]
        out = jnp.dot(ya.astype(BF16), wout_ref[:ATT_WIDTH, :], preferred_element_type=F32)
        return out + jnp.dot(yr.astype(BF16), wout_ref[ATT_WIDTH:, :], preferred_element_type=F32)

    def norm_and_route(r0, out):
        rs = slice(r0, r0 + chunk)
        x1 = _layer_norm(alpha * x_ref[rs, :] + out, lng_ref[...], lnb_ref[...])
        x1_ref[rs, :] = x1
        x1_hi = x1.astype(BF16)
        x1b_ref[rs, :] = x1_hi
        x1_lo = (x1 - x1_hi.astype(F32)).astype(BF16)
        o = (lax.dot_general(wr_ref[...], x1_hi, nt, preferred_element_type=F32)
             + lax.dot_general(wr_ref[...], x1_lo, nt, preferred_element_type=F32))
        logits_t = o[:rows, :] + o[rows:, :] + br_ref[...]
        comb_t, chosen = _route(logits_t)
        comb_parts.append(jnp.concatenate(comb_t, axis=0))
        chosen_parts.append(chosen)

    def emit_sort_tile(t):
        per = SORT_TILE // chunk
        chosen = jnp.concatenate(chosen_parts[t * per:(t + 1) * per], axis=1)
        pos, seg_start, seg_end = _sort_by_group(chosen, tri_ref)
        ts_ = slice(t * SORT_TILE, (t + 1) * SORT_TILE)
        pos_ref[0, :, ts_] = pos
        lane = lax.broadcasted_iota(jnp.int32, (SUBLANES, LANES), 1)
        seg_ref[0, t * SUBLANES:(t + 1) * SUBLANES, :] = jnp.where(
            lane == 0, seg_start, jnp.where(lane == 1, seg_end, 0.0))
        meta_row = lax.broadcasted_iota(jnp.int32, (SUBLANES, SORT_TILE), 0)
        meta = jnp.where(meta_row == 0, chosen.astype(F32), jnp.where(meta_row == 1, pos, 0.0))
        comb_t = jnp.concatenate([jnp.concatenate(comb_parts[t * per:(t + 1) * per], axis=1), meta,
                                  jnp.zeros((LANES - rows - SUBLANES, SORT_TILE), F32)], axis=0)
        comb_ref[ts_, :] = comb_t.T

    starts = list(range(0, tm, chunk))
    pending = project(starts[0])
    for k, r0 in enumerate(starts):
        nxt = project(starts[k + 1]) if k + 1 < len(starts) else None
        norm_and_route(r0, pending)
        pending = nxt
        if (r0 + chunk) % SORT_TILE == 0:
            emit_sort_tile(r0 // SORT_TILE)


def _mix_out(x2d, attn2d, rec2d, ng, wout, lng, lnb, wr, br, tri, alpha, tm):
    T, D = x2d.shape
    const = lambda i: (0, 0)
    tile = lambda i: (i, 0)
    n_tiles = T // tm
    sort_tiles = tm // SORT_TILE
    return pl.pallas_call(
        functools.partial(_mix_kernel, alpha=alpha),
        grid=(T // tm,),
        in_specs=[
            pl.BlockSpec((tm, D), tile),
            pl.BlockSpec((tm, ATT_WIDTH), tile),
            pl.BlockSpec((tm, LRU_WIDTH), tile),
            pl.BlockSpec(ng.shape, const),
            pl.BlockSpec(wout.shape, const),
            pl.BlockSpec(lng.shape, const),
            pl.BlockSpec(lnb.shape, const),
            pl.BlockSpec(wr.shape, const),
            pl.BlockSpec(br.shape, const),
            pl.BlockSpec(tri.shape, const),
        ],
        out_specs=[
            pl.BlockSpec((tm, D), tile),
            pl.BlockSpec((tm, D), tile),
            pl.BlockSpec((tm, LANES), tile),
            pl.BlockSpec((1, 1, tm), lambda i: (i, 0, 0)),
            pl.BlockSpec((1, sort_tiles * SUBLANES, LANES), lambda i: (i, 0, 0)),
        ],
        out_shape=[
            jax.ShapeDtypeStruct((T, D), F32),
            jax.ShapeDtypeStruct((T, D), BF16),
            jax.ShapeDtypeStruct((T, LANES), F32),
            jax.ShapeDtypeStruct((n_tiles, 1, tm), F32),
            jax.ShapeDtypeStruct((n_tiles, sort_tiles * SUBLANES, LANES), F32),
        ],
        compiler_params=_cparams(("parallel",)),
        name="mix_out_router",
    )(x2d, attn2d, rec2d, ng, wout, lng, lnb, wr, br, tri)


def _moe_kernel(seg_ref, x1_ref, xb_ref, comb_ref, pos_ref, wgu_ref, wd_ref, lng_ref, lnb_ref, o_ref,
                xs_ref, cs_ref, ys_ref, *, alpha):
    i = pl.program_id(0)
    g = pl.program_id(1)
    tm = xb_ref.shape[0]
    F = wd_ref.shape[1]
    sort_tiles = tm // SORT_TILE

    @pl.when(g == 0)
    def _():
        comb = comb_ref[...]
        hi = comb.astype(BF16)
        lo = (comb - hi.astype(F32)).astype(BF16)
        comb_hl = jnp.concatenate([hi, lo], axis=1)
        for r0 in range(0, tm, PERM_ROWS):
            t0 = (r0 // SORT_TILE) * SORT_TILE
            ts_ = slice(t0, t0 + SORT_TILE)
            pos = pos_ref[0, :, ts_].astype(jnp.int32)
            srow = (r0 - t0) + lax.broadcasted_iota(jnp.int32, (PERM_ROWS, SORT_TILE), 0)
            perm = jnp.where(srow == pos, 1.0, 0.0).astype(BF16)
            xs_ref[r0:r0 + PERM_ROWS, :] = jnp.dot(perm, xb_ref[ts_, :], preferred_element_type=F32).astype(BF16)
            c2 = jnp.dot(perm, comb_hl[ts_, :], preferred_element_type=F32)
            cs_ref[r0:r0 + PERM_ROWS, :] = c2[:, :LANES] + c2[:, LANES:]
        ys_ref[...] = jnp.zeros_like(ys_ref)

    lane = lax.broadcasted_iota(jnp.int32, (sort_tiles * MOE_CHUNK, LANES), 1)
    crow = lax.broadcasted_iota(jnp.int32, (MOE_CHUNK, 1), 0)
    bounds = []
    n_chunks = 0
    for t in range(sort_tiles):
        s_idx = ((i * sort_tiles + t) * N_GROUPS + g) * 2
        start = seg_ref[s_idx]
        end = seg_ref[s_idx + 1]
        base = (start // BF16_SUBLANES) * BF16_SUBLANES
        bounds.append((start, end, base))
        n_chunks = jnp.maximum(n_chunks, (end - base + MOE_CHUNK - 1) // MOE_CHUNK)

    def chunk(c, carry):
        windows, lives = [], []
        for t, (start, end, base) in enumerate(bounds):
            lo_row = base + c * MOE_CHUNK
            local = jnp.minimum(lo_row, SORT_TILE - MOE_CHUNK)
            windows.append(pl.ds(pl.multiple_of(t * SORT_TILE + local, BF16_SUBLANES), MOE_CHUNK))
            lrow = local + crow
            lives.append((lrow >= jnp.maximum(start, lo_row)) & (lrow < jnp.minimum(end, lo_row + MOE_CHUNK)))
        xs = jnp.concatenate([xs_ref[w, :] for w in windows], axis=0)
        cs = jnp.concatenate([cs_ref[w, :] for w in windows], axis=0)
        live = jnp.concatenate(lives, axis=0)
        acc = jnp.zeros((sort_tiles * MOE_CHUNK, o_ref.shape[1]), F32)
        for j in range(EXPERTS_PER_GROUP):
            gu = jnp.dot(xs, wgu_ref[j], preferred_element_type=F32)
            gate = gu[:, :F]
            w = jnp.sum(jnp.where(lane == SUBLANES * j + g, cs, 0.0), axis=1, keepdims=True)
            w = jnp.where(live, w, 0.0)
            act = gate * jax.nn.sigmoid(gate) * gu[:, F:] * w
            acc = acc + jnp.dot(act.astype(BF16), wd_ref[j], preferred_element_type=F32)
        for t, w in enumerate(windows):
            ys_ref[w, :] += acc[t * MOE_CHUNK:(t + 1) * MOE_CHUNK, :]
        return carry

    lax.fori_loop(0, n_chunks, chunk, 0)

    @pl.when(g == N_GROUPS - 1)
    def _():
        scol = lax.broadcasted_iota(jnp.int32, (PERM_ROWS, SORT_TILE), 1)
        for r0 in range(0, tm, PERM_ROWS):
            rs = slice(r0, r0 + PERM_ROWS)
            t0 = (r0 // SORT_TILE) * SORT_TILE
            pos_col = comb_ref[rs, ROUTE_ROWS + 1:ROUTE_ROWS + 2].astype(jnp.int32)
            perm = jnp.where(scol == pos_col, 1.0, 0.0).astype(BF16)
            y = jnp.dot(perm, ys_ref[t0:t0 + SORT_TILE, :].astype(BF16), preferred_element_type=F32)
            o_ref[rs, :] = _layer_norm(alpha * x1_ref[rs, :] + y, lng_ref[...], lnb_ref[...])


def _moe(seg, x1, x1b, comb, pos, wgu, wd, lng, lnb, alpha, tm):
    T, D = x1.shape
    F = wd.shape[1]
    const = lambda i, g, seg: (0, 0)
    tile = lambda i, g, seg: (i, 0)
    return pl.pallas_call(
        functools.partial(_moe_kernel, alpha=alpha),
        grid_spec=pltpu.PrefetchScalarGridSpec(
            num_scalar_prefetch=1,
            grid=(T // tm, N_GROUPS),
            in_specs=[
                pl.BlockSpec((tm, D), tile),
                pl.BlockSpec((tm, D), tile),
                pl.BlockSpec((tm, LANES), tile),
                pl.BlockSpec((1, 1, tm), lambda i, g, seg: (i, 0, 0)),
                pl.BlockSpec((EXPERTS_PER_GROUP, D, 2 * F), lambda i, g, seg: (g, 0, 0)),
                pl.BlockSpec((EXPERTS_PER_GROUP, F, D), lambda i, g, seg: (g, 0, 0)),
                pl.BlockSpec(lng.shape, const),
                pl.BlockSpec(lnb.shape, const),
            ],
            out_specs=pl.BlockSpec((tm, D), tile),
            scratch_shapes=[
                pltpu.VMEM((tm, D), BF16),
                pltpu.VMEM((tm, LANES), F32),
                pltpu.VMEM((tm, D), F32),
            ],
        ),
        out_shape=jax.ShapeDtypeStruct((T, D), F32),
        compiler_params=_cparams(("parallel", "arbitrary")),
        name="moe_ffn",
    )(seg, x1, x1b, comb, pos, wgu, wd, lng, lnb)


def _block_diag(w):
    n, c, d = w.shape
    return jnp.einsum('ncd,nm->ncmd', w, jnp.eye(n, dtype=w.dtype)).reshape(n * c, n * d)


def _router_operands(w_router, b_router):
    D = w_router.shape[0]
    wt = w_router.T.reshape(N_GROUPS, EXPERTS_PER_GROUP, D).transpose(1, 0, 2)
    wt = jnp.pad(wt, ((0, 0), (0, SUBLANES - N_GROUPS), (0, 0))).reshape(4 * SUBLANES, D)
    hi = wt.astype(BF16)
    lo = (wt - hi.astype(F32)).astype(BF16)
    bt = b_router.reshape(N_GROUPS, EXPERTS_PER_GROUP).T
    bt = jnp.pad(bt, ((0, 0), (0, SUBLANES - N_GROUPS)), constant_values=NEG_INF).reshape(4 * SUBLANES, 1)
    return jnp.concatenate([hi, lo], axis=0), bt.astype(F32)


def kernel(x, w_in, b_forget, conv_w, conv_b, w_rec_gate, b_rec_gate, w_in_gate, b_in_gate, lru_lambda,
           mix_norm_g, w_out, ln1_g, ln1_b, w_router, b_router, w_exp_gate, w_exp_up, w_exp_down,
           ln2_g, ln2_b):
    B, S, D = x.shape
    depth = w_in.shape[0]
    alpha = (2.0 * depth) ** 0.25
    T = B * S
    o_f = 3 * ATT_WIDTH
    o_x = o_f + N_HEADS

    blk_attn = min(512, S)
    tm_moe = min(1024, T)
    tri = jnp.triu(jnp.ones((SORT_TILE, SORT_TILE), F32)).astype(BF16)

    wr, br = _router_operands(w_router, b_router)
    row = lambda a: a.reshape(1, -1).astype(F32)
    vt_one = (jnp.arange(N_HEADS * VT_ROWS) % VT_ROWS == HEAD_DIM).astype(F32).reshape(-1, 1)

    for l in range(depth):
        wq = w_in[l, :, :ATT_WIDTH].astype(BF16)
        wk = w_in[l, :, ATT_WIDTH:2 * ATT_WIDTH].astype(BF16)
        wvt = w_in[l, :, 2 * ATT_WIDTH:o_f].T.reshape(N_HEADS, HEAD_DIM, D)
        wvt = jnp.pad(wvt, ((0, 0), (0, VT_ROWS - HEAD_DIM), (0, 0))).reshape(N_HEADS * VT_ROWS, D).astype(BF16)
        wf = jnp.pad(w_in[l, :, o_f:o_x], ((0, 0), (0, LANES - N_HEADS))).astype(BF16)
        bf = jnp.pad(b_forget[l], (0, LANES - N_HEADS)).reshape(1, LANES).astype(F32)
        wxg = w_in[l, :, o_x:].astype(BF16)
        wri = jnp.concatenate([_block_diag(w_rec_gate[l]), _block_diag(w_in_gate[l])], axis=1).astype(BF16)
        bri = jnp.concatenate([b_rec_gate[l], b_in_gate[l]]).reshape(1, -1).astype(F32)
        qq, kk, vt, rec = _in_proj(x, wq, wk, wvt, vt_one, wf, bf, wxg, conv_w[l].astype(F32), row(conv_b[l]),
                                   wri, bri, row(lru_lambda[l]), blk_attn)

        attn = _attention(qq, kk, vt, blk_attn)

        x1, x1b, comb, pos, seg = _mix_out(
            x.reshape(T, D), attn.reshape(T, ATT_WIDTH), rec.reshape(T, LRU_WIDTH), row(mix_norm_g[l]),
            w_out[l].astype(BF16), row(ln1_g[l]), row(ln1_b[l]), wr, br, tri, alpha, tm_moe)

        seg = seg.reshape(-1, SUBLANES, LANES)[:, :N_GROUPS, :2].astype(jnp.int32).reshape(-1)
        wgu = jnp.concatenate([w_exp_gate[l], w_exp_up[l]], axis=-1).astype(BF16)
        x = _moe(seg, x1, x1b, comb, pos, wgu, w_exp_down[l].astype(BF16), row(ln2_g[l]), row(ln2_b[l]),
                 alpha, tm_moe).reshape(B, S, D).astype(x.dtype)
    return x
```

```python
import functools

import jax
import jax.numpy as jnp
from jax import lax
from jax.experimental import pallas as pl
from jax.experimental.pallas import tpu as pltpu

F32 = jnp.float32
BF16 = jnp.bfloat16

HEAD_DIM = 64
N_HEADS = 8
ATT_WIDTH = N_HEADS * HEAD_DIM
LRU_WIDTH = 512
LRU_C = 8.0
CONV_W = 4
N_GROUPS = 4
EXPERTS_PER_GROUP = 4
N_EXPERTS = N_GROUPS * EXPERTS_PER_GROUP
LN_EPS = 1e-5
RMS_EPS = 1e-6
NEG_INF = -1e30

LANES = 128
SUBLANES = 8
HEADS_PER_BLOCK = LANES // HEAD_DIM
N_HEAD_BLOCKS = N_HEADS // HEADS_PER_BLOCK
BIAS_SLOT = 8
ATTN_PAIRS = 2
ATTN_HEADS = ATTN_PAIRS * HEADS_PER_BLOCK
BF16_SUBLANES = 16
VT_ROWS = HEAD_DIM + BF16_SUBLANES
LOG2E = 1.4426950408889634
GELU_K = 0.7978845608028654
VMEM_LIMIT = 48 * 1024 * 1024
MIX_CHUNK = 256
SORT_TILE = 512
MOE_CHUNK = 160
PERM_ROWS = 256
ROUTE_ROWS = 4 * SUBLANES


def _cparams(sem):
    return pltpu.CompilerParams(dimension_semantics=sem, vmem_limit_bytes=VMEM_LIMIT)


def _lru_tile(x, gate, cw_ref, cb_ref, wri_ref, bri_ref, lam_ref, ext_ref, h_ref):
    ts, W = x.shape
    ext_ref[SUBLANES:SUBLANES + ts, :] = x
    xc = cb_ref[...] + cw_ref[CONV_W - 1:CONV_W, :] * x
    for back in range(1, CONV_W):
        xc = xc + cw_ref[CONV_W - 1 - back:CONV_W - back, :] * ext_ref[SUBLANES - back:SUBLANES - back + ts, :]
    ext_ref[0:SUBLANES, :] = x[ts - SUBLANES:ts, :]

    ri = jnp.dot(xc.astype(BF16), wri_ref[...], preferred_element_type=F32) + bri_ref[...]
    r = 0.5 * jnp.tanh(0.5 * ri[:, :W]) + 0.5
    i = 0.5 * jnp.tanh(0.5 * ri[:, W:]) + 0.5
    neg_lam = -lam_ref[...]
    softplus = jnp.maximum(neg_lam, 0.0) + jnp.log1p(jnp.exp(-jnp.abs(neg_lam)))
    a = jnp.exp(-LRU_C * r * softplus)
    v = jnp.maximum(1.0 - a * a, 0.0)
    u = jnp.where(v > 0.0, v * lax.rsqrt(v), 0.0) * (i * xc)

    row = lax.broadcasted_iota(jnp.int32, (SUBLANES, W), 0)
    h_prev = h_ref[...]
    hs = []
    for j in range(ts // SUBLANES):
        A = a[j * SUBLANES:(j + 1) * SUBLANES, :]
        U = u[j * SUBLANES:(j + 1) * SUBLANES, :]
        for d in (1, 2, 4):
            keep = row >= d
            U = U + A * jnp.where(keep, pltpu.roll(U, d, axis=0), 0.0)
            A = A * jnp.where(keep, pltpu.roll(A, d, axis=0), 1.0)
        H = A * h_prev + U
        hs.append(H)
        h_prev = jnp.broadcast_to(H[SUBLANES - 1:SUBLANES, :], (SUBLANES, W))
    h_ref[...] = h_prev
    h = jnp.concatenate(hs, axis=0)

    inner = gate * (GELU_K + (GELU_K * 0.044715) * (gate * gate))
    half = 0.5 * gate
    return (half + half * jnp.tanh(inner)) * h


def _in_proj_kernel(x_ref, wq_ref, wk_ref, wvt_ref, vt_one_ref, wf_ref, bf_ref, wxg_ref,
                    cw_ref, cb_ref, wri_ref, bri_ref, lam_ref,
                    qq_ref, kk_ref, vt_ref, rec_ref, carry_ref, ext_ref, h_ref):
    tm = x_ref.shape[1]

    @pl.when(pl.program_id(1) == 0)
    def _():
        carry_ref[...] = jnp.zeros_like(carry_ref)
        ext_ref[0:SUBLANES, :] = jnp.zeros((SUBLANES, LRU_WIDTH), F32)
        h_ref[...] = jnp.zeros_like(h_ref)

    xb = x_ref[0].astype(BF16)
    xg = jnp.dot(xb, wxg_ref[...], preferred_element_type=F32)
    rec = _lru_tile(xg[:, :LRU_WIDTH], xg[:, LRU_WIDTH:], cw_ref, cb_ref, wri_ref, bri_ref, lam_ref,
                    ext_ref, h_ref)
    rec_ref[0] = rec.astype(rec_ref.dtype)

    f = jnp.dot(xb, wf_ref[...], preferred_element_type=F32) + bf_ref[...]

    c = jnp.minimum(f, 0.0) - jnp.log1p(jnp.exp(-jnp.abs(f)))
    row = lax.broadcasted_iota(jnp.int32, c.shape, 0)
    d = 1
    while d < tm:
        c = c + jnp.where(row >= d, pltpu.roll(c, d, axis=0), 0.0)
        d *= 2
    c = c + carry_ref[...]
    carry_ref[...] = c[tm - 1:tm, :]

    c2 = c * LOG2E
    parts = []
    for h in range(N_HEADS):
        col = c2[:, h:h + 1]
        hi = col.astype(BF16).astype(F32)
        rem = col - hi
        mid = rem.astype(BF16).astype(F32)
        parts.append((hi, mid, rem - mid))

    lane = lax.broadcasted_iota(jnp.int32, (tm, LANES), 1)
    slot = lane % BIAS_SLOT
    head_slot = lane // BIAS_SLOT
    valid = head_slot < HEADS_PER_BLOCK
    for hb in range(N_HEAD_BLOCKS):
        def part(k):
            return jnp.where(head_slot == 0, parts[2 * hb][k], parts[2 * hb + 1][k])

        cq = jnp.where(slot == 0, part(0), jnp.where(slot == 1, part(1), part(2)))
        ck = jnp.where(slot == 3, part(0), jnp.where(slot == 4, part(1), part(2)))
        qx = jnp.where(valid & (slot < 3), cq, jnp.where(valid & (slot < 6), 1.0, 0.0))
        kx = jnp.where(valid & (slot < 3), 1.0, jnp.where(valid & (slot < 6), -ck, 0.0))
        qq_ref[0, :, (2 * hb + 1) * LANES:(2 * hb + 2) * LANES] = qx.astype(BF16)
        kk_ref[0, :, (2 * hb + 1) * LANES:(2 * hb + 2) * LANES] = kx.astype(BF16)

    vt = lax.dot_general(wvt_ref[...], xb, (((1,), (1,)), ((), ())), preferred_element_type=F32)
    vt_ref[0, 0] = (vt + vt_one_ref[...]).astype(BF16)

    q = jnp.dot(xb, wq_ref[...], preferred_element_type=F32) * (HEAD_DIM ** -0.5 * LOG2E)
    k = jnp.dot(xb, wk_ref[...], preferred_element_type=F32)
    for hb in range(N_HEAD_BLOCKS):
        qq_ref[0, :, 2 * hb * LANES:(2 * hb + 1) * LANES] = q[:, hb * LANES:(hb + 1) * LANES].astype(BF16)
        kk_ref[0, :, 2 * hb * LANES:(2 * hb + 1) * LANES] = k[:, hb * LANES:(hb + 1) * LANES].astype(BF16)


def _in_proj(x, wq, wk, wvt, vt_one, wf, bf, wxg, cw, cb, wri, bri, lam, tm):
    B, S, D = x.shape
    const = lambda b, s: (0, 0)
    tile = lambda b, s: (b, s, 0)
    weights = (wq, wk, wvt, vt_one, wf, bf, wxg, cw, cb, wri, bri, lam)
    return pl.pallas_call(
        _in_proj_kernel,
        grid=(B, S // tm),
        in_specs=[pl.BlockSpec((1, tm, D), tile)] + [pl.BlockSpec(w.shape, const) for w in weights],
        out_specs=[
            pl.BlockSpec((1, tm, 2 * ATT_WIDTH), tile),
            pl.BlockSpec((1, tm, 2 * ATT_WIDTH), tile),
            pl.BlockSpec((1, 1, N_HEADS * VT_ROWS, tm), lambda b, s: (b, s, 0, 0)),
            pl.BlockSpec((1, tm, LRU_WIDTH), tile),
        ],
        out_shape=[
            jax.ShapeDtypeStruct((B, S, 2 * ATT_WIDTH), BF16),
            jax.ShapeDtypeStruct((B, S, 2 * ATT_WIDTH), BF16),
            jax.ShapeDtypeStruct((B, S // tm, N_HEADS * VT_ROWS, tm), BF16),
            jax.ShapeDtypeStruct((B, S, LRU_WIDTH), BF16),
        ],
        scratch_shapes=[
            pltpu.VMEM((1, LANES), F32),
            pltpu.VMEM((tm + SUBLANES, LRU_WIDTH), F32),
            pltpu.VMEM((SUBLANES, LRU_WIDTH), F32),
        ],
        compiler_params=_cparams(("parallel", "arbitrary")),
        name="in_proj_lru",
    )(x, *weights)


def _attn_kernel(qq_ref, kk_ref, vt_ref, o_ref, st0_ref, st1_ref, bm0_ref, bm1_ref, m_ref, acc_ref, *, blk):
    qi = pl.program_id(2)
    st_refs = (st0_ref, st1_ref)
    bm_refs = (bm0_ref, bm1_ref)
    lane = lax.broadcasted_iota(jnp.int32, (blk, 2 * LANES), 1)
    q_heads = []
    for pair in range(ATTN_PAIRS):
        q = qq_ref[0, :, pair * 2 * LANES:(pair + 1) * 2 * LANES]
        zero = jnp.zeros_like(q)
        for h in range(HEADS_PER_BLOCK):
            own = ((lane >= h * HEAD_DIM) & (lane < (h + 1) * HEAD_DIM)) | (
                (lane >= LANES + h * BIAS_SLOT) & (lane < LANES + (h + 1) * BIAS_SLOT))
            q_heads.append(jnp.where(own, q, zero))

    m_ref[...] = jnp.full_like(m_ref, NEG_INF)
    acc_ref[...] = jnp.zeros_like(acc_ref)

    def scores(ki, h, slot, masked):
        pair = h // HEADS_PER_BLOCK
        k_blk = kk_ref[0, pl.ds(pl.multiple_of(ki * blk, blk), blk),
                       pair * 2 * LANES:(pair + 1) * 2 * LANES]
        st = lax.dot_general(k_blk, q_heads[h], (((1,), (1,)), ((), ())),
                             preferred_element_type=F32)
        if masked:
            key = lax.broadcasted_iota(jnp.int32, (blk, blk), 0)
            qry = lax.broadcasted_iota(jnp.int32, (blk, blk), 1)
            st = jnp.where(key - qry <= (qi - ki) * blk, st, NEG_INF)
        st_refs[slot][h] = st
        bm_refs[slot][h] = jnp.max(st, axis=0, keepdims=True)

    def update(ki, h, slot):
        m_old = m_ref[h]
        m_new = jnp.maximum(m_old, bm_refs[slot][h])
        alpha = jnp.exp2(m_old - m_new)
        p = jnp.exp2(st_refs[slot][h] - m_new).astype(BF16)
        vt = vt_ref[0, ki, h * VT_ROWS:(h + 1) * VT_ROWS, :]
        acc_ref[h] = alpha * acc_ref[h] + jnp.dot(vt, p, preferred_element_type=F32)
        m_ref[h] = m_new

    def stage(score_blk, score_slot, masked, update_blk, update_slot):
        for h in range(ATTN_HEADS):
            if score_blk is not None:
                scores(score_blk, h, score_slot, masked)
            if update_blk is not None:
                update(update_blk, h, update_slot)

    stage(0, 0, True, None, None)

    def body(j, carry):
        i = 2 * j
        stage(i + 1, 1, False, i, 0)
        stage(i + 2, 0, False, i + 1, 1)
        return carry

    lax.fori_loop(0, (qi - 1) // 2, body, 0)

    @pl.when(qi == 0)
    def _():
        stage(None, None, False, 0, 0)

    @pl.when(qi % 2 == 1)
    def _():
        stage(qi, 1, True, qi - 1, 0)
        stage(None, None, False, qi, 1)

    @pl.when((qi % 2 == 0) & (qi > 0))
    def _():
        stage(qi - 1, 1, False, qi - 2, 0)
        stage(qi, 0, True, qi - 1, 1)
        stage(None, None, False, qi, 0)

    ot = jnp.concatenate([acc_ref[h, :HEAD_DIM, :] / acc_ref[h, HEAD_DIM:HEAD_DIM + 1, :]
                          for h in range(ATTN_HEADS)], axis=0)
    o_ref[0] = ot.T.astype(o_ref.dtype)


def _attention(qq, kk, vt, blk):
    B, nblk, _, _ = vt.shape
    S = nblk * blk
    return pl.pallas_call(
        functools.partial(_attn_kernel, blk=blk),
        grid=(B, N_HEAD_BLOCKS // ATTN_PAIRS, nblk),
        in_specs=[
            pl.BlockSpec((1, blk, ATTN_PAIRS * 2 * LANES), lambda b, h, i: (b, i, h)),
            pl.BlockSpec((1, S, ATTN_PAIRS * 2 * LANES), lambda b, h, i: (b, 0, h)),
            pl.BlockSpec((1, nblk, ATTN_HEADS * VT_ROWS, blk), lambda b, h, i: (b, 0, h, 0)),
        ],
        out_specs=pl.BlockSpec((1, blk, ATTN_HEADS * HEAD_DIM), lambda b, h, i: (b, i, h)),
        out_shape=jax.ShapeDtypeStruct((B, S, ATT_WIDTH), BF16),
        scratch_shapes=[
            pltpu.VMEM((ATTN_HEADS, blk, blk), F32),
            pltpu.VMEM((ATTN_HEADS, blk, blk), F32),
            pltpu.VMEM((ATTN_HEADS, 1, blk), F32),
            pltpu.VMEM((ATTN_HEADS, 1, blk), F32),
            pltpu.VMEM((ATTN_HEADS, 1, blk), F32),
            pltpu.VMEM((ATTN_HEADS, VT_ROWS, blk), F32),
        ],
        compiler_params=_cparams(("parallel", "parallel", "arbitrary")),
        name="fox_attention",
    )(qq, kk, vt)


def _layer_norm(y, g, b):
    mu = jnp.mean(y, axis=-1, keepdims=True)
    yc = y - mu
    var = jnp.mean(yc * yc, axis=-1, keepdims=True)
    return yc * lax.rsqrt(var + LN_EPS) * g + b


def _route(logits_t):
    mx = jnp.max(logits_t, axis=0, keepdims=True)
    e = jnp.exp(logits_t - mx)
    p = e / jnp.sum(e, axis=0, keepdims=True)
    v = [p[SUBLANES * j:SUBLANES * (j + 1), :] for j in range(EXPERTS_PER_GROUP)]

    def first_max(vals):
        best = functools.reduce(jnp.maximum, vals)
        idx = jnp.full(best.shape, len(vals) - 1, jnp.int32)
        for j in range(len(vals) - 2, -1, -1):
            idx = jnp.where(vals[j] == best, j, idx)
        return best, idx

    m1, idx1 = first_max(v)
    m2, idx2 = first_max([jnp.where(idx1 == j, -1.0, v[j]) for j in range(EXPERTS_PER_GROUP)])
    grp = lax.broadcasted_iota(jnp.int32, m1.shape, 0)
    real = grp < N_GROUPS
    score = jnp.where(real, m1 + m2, -1.0)
    best = jnp.max(score, axis=0, keepdims=True)
    chosen = jnp.min(jnp.where(score == best, grp, SUBLANES), axis=0, keepdims=True)
    sel = grp == chosen
    den = jnp.where(real, m1 + m2, 1.0)
    g1 = m1 / den
    g2 = m2 / den
    comb = [jnp.where(sel, jnp.where(idx1 == j, g1, 0.0) + jnp.where(idx2 == j, g2, 0.0), 0.0)
            for j in range(EXPERTS_PER_GROUP)]
    return comb, chosen


def _sort_by_group(chosen, tri_ref):
    tm = chosen.shape[1]
    grp = lax.broadcasted_iota(jnp.int32, (SUBLANES, tm), 0)
    member = grp == chosen
    onehot = jnp.where(member, 1.0, 0.0).astype(BF16)
    count = jnp.dot(onehot, tri_ref[...], preferred_element_type=F32)
    total = count[:, tm - 1:tm]
    gcol = lax.broadcasted_iota(jnp.int32, (SUBLANES, 1), 0)
    start = jnp.zeros((SUBLANES, 1), F32)
    for k in range(N_GROUPS - 1):
        start = start + jnp.where(gcol > k, total[k:k + 1, :], 0.0)
    pos = jnp.sum(jnp.where(member, start + count - 1.0, 0.0), axis=0, keepdims=True)
    return pos, start, start + total


def _mix_kernel(x_ref, attn_ref, rec_ref, ng_ref, wout_ref, lng_ref, lnb_ref, wr_ref, br_ref, tri_ref,
                x1_ref, x1b_ref, comb_ref, pos_ref, seg_ref, *, alpha):
    tm = x_ref.shape[0]
    chunk = min(MIX_CHUNK, tm)
    rows = ROUTE_ROWS
    nt = (((1,), (1,)), ((), ()))
    comb_parts, chosen_parts = [], []

    def project(r0):
        rs = slice(r0, r0 + chunk)
        ya = attn_ref[rs, :].astype(F32)
        yr = rec_ref[rs, :].astype(F32)
        ya = ya * lax.rsqrt(jnp.mean(ya * ya, axis=-1, keepdims=True) + RMS_EPS) * ng_ref[:, :ATT_WIDTH]
        yr = yr * lax.rsqrt(jnp.mean(yr * yr, axis=-1, keepdims=True) + RMS_EPS) * ng_ref[:, ATT_WIDTH:]
        out = jnp.dot(ya.astype(BF16), wout_ref[:ATT_WIDTH, :], preferred_element_type=F32)
        return out + jnp.dot(yr.astype(BF16), wout_ref[ATT_WIDTH:, :], preferred_element_type=F32)

    def norm_and_route(r0, out):
        rs = slice(r0, r0 + chunk)
        x1 = _layer_norm(alpha * x_ref[rs, :] + out, lng_ref[...], lnb_ref[...])
        x1_ref[rs, :] = x1
        x1_hi = x1.astype(BF16)
        x1b_ref[rs, :] = x1_hi
        x1_lo = (x1 - x1_hi.astype(F32)).astype(BF16)
        o = (lax.dot_general(wr_ref[...], x1_hi, nt, preferred_element_type=F32)
             + lax.dot_general(wr_ref[...], x1_lo, nt, preferred_element_type=F32))
        logits_t = o[:rows, :] + o[rows:, :] + br_ref[...]
        comb_t, chosen = _route(logits_t)
        comb_parts.append(jnp.concatenate(comb_t, axis=0))
        chosen_parts.append(chosen)

    def emit_sort_tile(t):
        per = SORT_TILE // chunk
        chosen = jnp.concatenate(chosen_parts[t * per:(t + 1) * per], axis=1)
        pos, seg_start, seg_end = _sort_by_group(chosen, tri_ref)
        ts_ = slice(t * SORT_TILE, (t + 1) * SORT_TILE)
        pos_ref[0, :, ts_] = pos
        lane = lax.broadcasted_iota(jnp.int32, (SUBLANES, LANES), 1)
        seg_ref[0, t * SUBLANES:(t + 1) * SUBLANES, :] = jnp.where(
            lane == 0, seg_start, jnp.where(lane == 1, seg_end, 0.0))
        meta_row = lax.broadcasted_iota(jnp.int32, (SUBLANES, SORT_TILE), 0)
        meta = jnp.where(meta_row == 0, chosen.astype(F32), jnp.where(meta_row == 1, pos, 0.0))
        comb_t = jnp.concatenate([jnp.concatenate(comb_parts[t * per:(t + 1) * per], axis=1), meta,
                                  jnp.zeros((LANES - rows - SUBLANES, SORT_TILE), F32)], axis=0)
        comb_ref[ts_, :] = comb_t.T

    starts = list(range(0, tm, chunk))
    pending = project(starts[0])
    for k, r0 in enumerate(starts):
        nxt = project(starts[k + 1]) if k + 1 < len(starts) else None
        norm_and_route(r0, pending)
        pending = nxt
        if (r0 + chunk) % SORT_TILE == 0:
            emit_sort_tile(r0 // SORT_TILE)


def _mix_out(x2d, attn2d, rec2d, ng, wout, lng, lnb, wr, br, tri, alpha, tm):
    T, D = x2d.shape
    const = lambda i: (0, 0)
    tile = lambda i: (i, 0)
    n_tiles = T // tm
    sort_tiles = tm // SORT_TILE
    return pl.pallas_call(
        functools.partial(_mix_kernel, alpha=alpha),
        grid=(T // tm,),
        in_specs=[
            pl.BlockSpec((tm, D), tile),
            pl.BlockSpec((tm, ATT_WIDTH), tile),
            pl.BlockSpec((tm, LRU_WIDTH), tile),
            pl.BlockSpec(ng.shape, const),
            pl.BlockSpec(wout.shape, const),
            pl.BlockSpec(lng.shape, const),
            pl.BlockSpec(lnb.shape, const),
            pl.BlockSpec(wr.shape, const),
            pl.BlockSpec(br.shape, const),
            pl.BlockSpec(tri.shape, const),
        ],
        out_specs=[
            pl.BlockSpec((tm, D), tile),
            pl.BlockSpec((tm, D), tile),
            pl.BlockSpec((tm, LANES), tile),
            pl.BlockSpec((1, 1, tm), lambda i: (i, 0, 0)),
            pl.BlockSpec((1, sort_tiles * SUBLANES, LANES), lambda i: (i, 0, 0)),
        ],
        out_shape=[
            jax.ShapeDtypeStruct((T, D), F32),
            jax.ShapeDtypeStruct((T, D), BF16),
            jax.ShapeDtypeStruct((T, LANES), F32),
            jax.ShapeDtypeStruct((n_tiles, 1, tm), F32),
            jax.ShapeDtypeStruct((n_tiles, sort_tiles * SUBLANES, LANES), F32),
        ],
        compiler_params=_cparams(("parallel",)),
        name="mix_out_router",
    )(x2d, attn2d, rec2d, ng, wout, lng, lnb, wr, br, tri)


def _moe_kernel(seg_ref, x1_ref, xb_ref, comb_ref, pos_ref, wgu_ref, wd_ref, lng_ref, lnb_ref, o_ref,
                xs_ref, cs_ref, ys_ref, *, alpha):
    i = pl.program_id(0)
    g = pl.program_id(1)
    tm = xb_ref.shape[0]
    F = wd_ref.shape[1]
    sort_tiles = tm // SORT_TILE

    @pl.when(g == 0)
    def _():
        comb = comb_ref[...]
        hi = comb.astype(BF16)
        lo = (comb - hi.astype(F32)).astype(BF16)
        comb_hl = jnp.concatenate([hi, lo], axis=1)
        for r0 in range(0, tm, PERM_ROWS):
            t0 = (r0 // SORT_TILE) * SORT_TILE
            ts_ = slice(t0, t0 + SORT_TILE)
            pos = pos_ref[0, :, ts_].astype(jnp.int32)
            srow = (r0 - t0) + lax.broadcasted_iota(jnp.int32, (PERM_ROWS, SORT_TILE), 0)
            perm = jnp.where(srow == pos, 1.0, 0.0).astype(BF16)
            xs_ref[r0:r0 + PERM_ROWS, :] = jnp.dot(perm, xb_ref[ts_, :], preferred_element_type=F32).astype(BF16)
            c2 = jnp.dot(perm, comb_hl[ts_, :], preferred_element_type=F32)
            cs_ref[r0:r0 + PERM_ROWS, :] = c2[:, :LANES] + c2[:, LANES:]
        ys_ref[...] = jnp.zeros_like(ys_ref)

    lane = lax.broadcasted_iota(jnp.int32, (sort_tiles * MOE_CHUNK, LANES), 1)
    crow = lax.broadcasted_iota(jnp.int32, (MOE_CHUNK, 1), 0)
    bounds = []
    n_chunks = 0
    for t in range(sort_tiles):
        s_idx = ((i * sort_tiles + t) * N_GROUPS + g) * 2
        start = seg_ref[s_idx]
        end = seg_ref[s_idx + 1]
        base = (start // BF16_SUBLANES) * BF16_SUBLANES
        bounds.append((start, end, base))
        n_chunks = jnp.maximum(n_chunks, (end - base + MOE_CHUNK - 1) // MOE_CHUNK)

    def chunk(c, carry):
        windows, lives = [], []
        for t, (start, end, base) in enumerate(bounds):
            lo_row = base + c * MOE_CHUNK
            local = jnp.minimum(lo_row, SORT_TILE - MOE_CHUNK)
            windows.append(pl.ds(pl.multiple_of(t * SORT_TILE + local, BF16_SUBLANES), MOE_CHUNK))
            lrow = local + crow
            lives.append((lrow >= jnp.maximum(start, lo_row)) & (lrow < jnp.minimum(end, lo_row + MOE_CHUNK)))
        xs = jnp.concatenate([xs_ref[w, :] for w in windows], axis=0)
        cs = jnp.concatenate([cs_ref[w, :] for w in windows], axis=0)
        live = jnp.concatenate(lives, axis=0)
        acc = jnp.zeros((sort_tiles * MOE_CHUNK, o_ref.shape[1]), F32)
        for j in range(EXPERTS_PER_GROUP):
            gu = jnp.dot(xs, wgu_ref[j], preferred_element_type=F32)
            gate = gu[:, :F]
            w = jnp.sum(jnp.where(lane == SUBLANES * j + g, cs, 0.0), axis=1, keepdims=True)
            w = jnp.where(live, w, 0.0)
            act = gate * jax.nn.sigmoid(gate) * gu[:, F:] * w
            acc = acc + jnp.dot(act.astype(BF16), wd_ref[j], preferred_element_type=F32)
        for t, w in enumerate(windows):
            ys_ref[w, :] += acc[t * MOE_CHUNK:(t + 1) * MOE_CHUNK, :]
        return carry

    lax.fori_loop(0, n_chunks, chunk, 0)

    @pl.when(g == N_GROUPS - 1)
    def _():
        scol = lax.broadcasted_iota(jnp.int32, (PERM_ROWS, SORT_TILE), 1)
        for r0 in range(0, tm, PERM_ROWS):
            rs = slice(r0, r0 + PERM_ROWS)
            t0 = (r0 // SORT_TILE) * SORT_TILE
            pos_col = comb_ref[rs, ROUTE_ROWS + 1:ROUTE_ROWS + 2].astype(jnp.int32)
            perm = jnp.where(scol == pos_col, 1.0, 0.0).astype(BF16)
            y = jnp.dot(perm, ys_ref[t0:t0 + SORT_TILE, :].astype(BF16), preferred_element_type=F32)
            o_ref[rs, :] = _layer_norm(alpha * x1_ref[rs, :] + y, lng_ref[...], lnb_ref[...])


def _moe(seg, x1, x1b, comb, pos, wgu, wd, lng, lnb, alpha, tm):
    T, D = x1.shape
    F = wd.shape[1]
    const = lambda i, g, seg: (0, 0)
    tile = lambda i, g, seg: (i, 0)
    return pl.pallas_call(
        functools.partial(_moe_kernel, alpha=alpha),
        grid_spec=pltpu.PrefetchScalarGridSpec(
            num_scalar_prefetch=1,
            grid=(T // tm, N_GROUPS),
            in_specs=[
                pl.BlockSpec((tm, D), tile),
                pl.BlockSpec((tm, D), tile),
                pl.BlockSpec((tm, LANES), tile),
                pl.BlockSpec((1, 1, tm), lambda i, g, seg: (i, 0, 0)),
                pl.BlockSpec((EXPERTS_PER_GROUP, D, 2 * F), lambda i, g, seg: (g, 0, 0)),
                pl.BlockSpec((EXPERTS_PER_GROUP, F, D), lambda i, g, seg: (g, 0, 0)),
                pl.BlockSpec(lng.shape, const),
                pl.BlockSpec(lnb.shape, const),
            ],
            out_specs=pl.BlockSpec((tm, D), tile),
            scratch_shapes=[
                pltpu.VMEM((tm, D), BF16),
                pltpu.VMEM((tm, LANES), F32),
                pltpu.VMEM((tm, D), F32),
            ],
        ),
        out_shape=jax.ShapeDtypeStruct((T, D), F32),
        compiler_params=_cparams(("parallel", "arbitrary")),
        name="moe_ffn",
    )(seg, x1, x1b, comb, pos, wgu, wd, lng, lnb)


def _block_diag(w):
    n, c, d = w.shape
    return jnp.einsum('ncd,nm->ncmd', w, jnp.eye(n, dtype=w.dtype)).reshape(n * c, n * d)


def _router_operands(w_router, b_router):
    D = w_router.shape[0]
    wt = w_router.T.reshape(N_GROUPS, EXPERTS_PER_GROUP, D).transpose(1, 0, 2)
    wt = jnp.pad(wt, ((0, 0), (0, SUBLANES - N_GROUPS), (0, 0))).reshape(4 * SUBLANES, D)
    hi = wt.astype(BF16)
    lo = (wt - hi.astype(F32)).astype(BF16)
    bt = b_router.reshape(N_GROUPS, EXPERTS_PER_GROUP).T
    bt = jnp.pad(bt, ((0, 0), (0, SUBLANES - N_GROUPS)), constant_values=NEG_INF).reshape(4 * SUBLANES, 1)
    return jnp.concatenate([hi, lo], axis=0), bt.astype(F32)


def kernel(x, w_in, b_forget, conv_w, conv_b, w_rec_gate, b_rec_gate, w_in_gate, b_in_gate, lru_lambda,
           mix_norm_g, w_out, ln1_g, ln1_b, w_router, b_router, w_exp_gate, w_exp_up, w_exp_down,
           ln2_g, ln2_b):
    B, S, D = x.shape
    depth = w_in.shape[0]
    alpha = (2.0 * depth) ** 0.25
    T = B * S
    o_f = 3 * ATT_WIDTH
    o_x = o_f + N_HEADS

    blk_attn = min(512, S)
    tm_moe = min(1024, T)
    tri = jnp.triu(jnp.ones((SORT_TILE, SORT_TILE), F32)).astype(BF16)

    wr, br = _router_operands(w_router, b_router)
    row = lambda a: a.reshape(1, -1).astype(F32)
    vt_one = (jnp.arange(N_HEADS * VT_ROWS) % VT_ROWS == HEAD_DIM).astype(F32).reshape(-1, 1)

    for l in range(depth):
        wq = w_in[l, :, :ATT_WIDTH].astype(BF16)
        wk = w_in[l, :, ATT_WIDTH:2 * ATT_WIDTH].astype(BF16)
        wvt = w_in[l, :, 2 * ATT_WIDTH:o_f].T.reshape(N_HEADS, HEAD_DIM, D)
        wvt = jnp.pad(wvt, ((0, 0), (0, VT_ROWS - HEAD_DIM), (0, 0))).reshape(N_HEADS * VT_ROWS, D).astype(BF16)
        wf = jnp.pad(w_in[l, :, o_f:o_x], ((0, 0), (0, LANES - N_HEADS))).astype(BF16)
        bf = jnp.pad(b_forget[l], (0, LANES - N_HEADS)).reshape(1, LANES).astype(F32)
        wxg = w_in[l, :, o_x:].astype(BF16)
        wri = jnp.concatenate([_block_diag(w_rec_gate[l]), _block_diag(w_in_gate[l])], axis=1).astype(BF16)
        bri = jnp.concatenate([b_rec_gate[l], b_in_gate[l]]).reshape(1, -1).astype(F32)
        qq, kk, vt, rec = _in_proj(x, wq, wk, wvt, vt_one, wf, bf, wxg, conv_w[l].astype(F32), row(conv_b[l]),
                                   wri, bri, row(lru_lambda[l]), blk_attn)

        attn = _attention(qq, kk, vt, blk_attn)

        x1, x1b, comb, pos, seg = _mix_out(
            x.reshape(T, D), attn.reshape(T, ATT_WIDTH), rec.reshape(T, LRU_WIDTH), row(mix_norm_g[l]),
            w_out[l].astype(BF16), row(ln1_g[l]), row(ln1_b[l]), wr, br, tri, alpha, tm_moe)

        seg = seg.reshape(-1, SUBLANES, LANES)[:, :N_GROUPS, :2].astype(jnp.int32).reshape(-1)
        wgu = jnp.concatenate([w_exp_gate[l], w_exp_up[l]], axis=-1).astype(BF16)
        x = _moe(seg, x1, x1b, comb, pos, wgu, w_exp_down[l].astype(BF16), row(ln2_g[l]), row(ln2_b[l]),
                 alpha, tm_moe).reshape(B, S, D).astype(x.dtype)
    return x
```

```python
import functools

import jax
import jax.numpy as jnp
from jax import lax
from jax.experimental import pallas as pl
from jax.experimental.pallas import tpu as pltpu

F32 = jnp.float32
BF16 = jnp.bfloat16

HEAD_DIM = 64
N_HEADS = 8
ATT_WIDTH = N_HEADS * HEAD_DIM
LRU_WIDTH = 512
LRU_C = 8.0
CONV_W = 4
N_GROUPS = 4
EXPERTS_PER_GROUP = 4
N_EXPERTS = N_GROUPS * EXPERTS_PER_GROUP
LN_EPS = 1e-5
RMS_EPS = 1e-6
NEG_INF = -1e30

LANES = 128
SUBLANES = 8
HEADS_PER_BLOCK = LANES // HEAD_DIM
N_HEAD_BLOCKS = N_HEADS // HEADS_PER_BLOCK
BIAS_SLOT = 8
ATTN_PAIRS = 4
ATTN_HEADS = ATTN_PAIRS * HEADS_PER_BLOCK
BF16_SUBLANES = 16
VT_ROWS = HEAD_DIM + BF16_SUBLANES
LOG2E = 1.4426950408889634
GELU_K = 0.7978845608028654
VMEM_LIMIT = 56 * 1024 * 1024
MIX_CHUNK = 256
SORT_TILE = 512
MOE_CHUNK = 160
PERM_ROWS = 256
ROUTE_ROWS = 4 * SUBLANES


def _cparams(sem):
    return pltpu.CompilerParams(dimension_semantics=sem, vmem_limit_bytes=VMEM_LIMIT)


def _lru_tile(x, gate, cw_ref, cb_ref, wri_ref, bri_ref, lam_ref, ext_ref, h_ref):
    ts, W = x.shape
    ext_ref[SUBLANES:SUBLANES + ts, :] = x
    xc = cb_ref[...] + cw_ref[CONV_W - 1:CONV_W, :] * x
    for back in range(1, CONV_W):
        xc = xc + cw_ref[CONV_W - 1 - back:CONV_W - back, :] * ext_ref[SUBLANES - back:SUBLANES - back + ts, :]
    ext_ref[0:SUBLANES, :] = x[ts - SUBLANES:ts, :]

    ri = jnp.dot(xc.astype(BF16), wri_ref[...], preferred_element_type=F32) + bri_ref[...]
    r = 0.5 * jnp.tanh(0.5 * ri[:, :W]) + 0.5
    i = 0.5 * jnp.tanh(0.5 * ri[:, W:]) + 0.5
    neg_lam = -lam_ref[...]
    softplus = jnp.maximum(neg_lam, 0.0) + jnp.log1p(jnp.exp(-jnp.abs(neg_lam)))
    a = jnp.exp(-LRU_C * r * softplus)
    v = jnp.maximum(1.0 - a * a, 0.0)
    u = jnp.where(v > 0.0, v * lax.rsqrt(v), 0.0) * (i * xc)

    row = lax.broadcasted_iota(jnp.int32, (SUBLANES, W), 0)
    h_prev = h_ref[...]
    hs = []
    for j in range(ts // SUBLANES):
        A = a[j * SUBLANES:(j + 1) * SUBLANES, :]
        U = u[j * SUBLANES:(j + 1) * SUBLANES, :]
        for d in (1, 2, 4):
            keep = row >= d
            U = U + A * jnp.where(keep, pltpu.roll(U, d, axis=0), 0.0)
            A = A * jnp.where(keep, pltpu.roll(A, d, axis=0), 1.0)
        H = A * h_prev + U
        hs.append(H)
        h_prev = jnp.broadcast_to(H[SUBLANES - 1:SUBLANES, :], (SUBLANES, W))
    h_ref[...] = h_prev
    h = jnp.concatenate(hs, axis=0)

    inner = gate * (GELU_K + (GELU_K * 0.044715) * (gate * gate))
    half = 0.5 * gate
    return (half + half * jnp.tanh(inner)) * h


def _in_proj_kernel(x_ref, wq_ref, wk_ref, wvt_ref, vt_one_ref, wf_ref, bf_ref, wxg_ref,
                    cw_ref, cb_ref, wri_ref, bri_ref, lam_ref,
                    qq_ref, kk_ref, vt_ref, rec_ref, carry_ref, ext_ref, h_ref):
    tm = x_ref.shape[1]

    @pl.when(pl.program_id(1) == 0)
    def _():
        carry_ref[...] = jnp.zeros_like(carry_ref)
        ext_ref[0:SUBLANES, :] = jnp.zeros((SUBLANES, LRU_WIDTH), F32)
        h_ref[...] = jnp.zeros_like(h_ref)

    xb = x_ref[0].astype(BF16)
    xg = jnp.dot(xb, wxg_ref[...], preferred_element_type=F32)
    rec = _lru_tile(xg[:, :LRU_WIDTH], xg[:, LRU_WIDTH:], cw_ref, cb_ref, wri_ref, bri_ref, lam_ref,
                    ext_ref, h_ref)
    rec_ref[0] = rec.astype(rec_ref.dtype)

    f = jnp.dot(xb, wf_ref[...], preferred_element_type=F32) + bf_ref[...]

    c = jnp.minimum(f, 0.0) - jnp.log1p(jnp.exp(-jnp.abs(f)))
    row = lax.broadcasted_iota(jnp.int32, c.shape, 0)
    d = 1
    while d < tm:
        c = c + jnp.where(row >= d, pltpu.roll(c, d, axis=0), 0.0)
        d *= 2
    c = c + carry_ref[...]
    carry_ref[...] = c[tm - 1:tm, :]

    c2 = c * LOG2E
    parts = []
    for h in range(N_HEADS):
        col = c2[:, h:h + 1]
        hi = col.astype(BF16).astype(F32)
        rem = col - hi
        mid = rem.astype(BF16).astype(F32)
        parts.append((hi, mid, rem - mid))

    lane = lax.broadcasted_iota(jnp.int32, (tm, LANES), 1)
    slot = lane % BIAS_SLOT
    head_slot = lane // BIAS_SLOT
    valid = head_slot < HEADS_PER_BLOCK
    for hb in range(N_HEAD_BLOCKS):
        def part(k):
            return jnp.where(head_slot == 0, parts[2 * hb][k], parts[2 * hb + 1][k])

        cq = jnp.where(slot == 0, part(0), jnp.where(slot == 1, part(1), part(2)))
        ck = jnp.where(slot == 3, part(0), jnp.where(slot == 4, part(1), part(2)))
        qx = jnp.where(valid & (slot < 3), cq, jnp.where(valid & (slot < 6), 1.0, 0.0))
        kx = jnp.where(valid & (slot < 3), 1.0, jnp.where(valid & (slot < 6), -ck, 0.0))
        qq_ref[0, :, (2 * hb + 1) * LANES:(2 * hb + 2) * LANES] = qx.astype(BF16)
        kk_ref[0, :, (2 * hb + 1) * LANES:(2 * hb + 2) * LANES] = kx.astype(BF16)

    vt = lax.dot_general(wvt_ref[...], xb, (((1,), (1,)), ((), ())), preferred_element_type=F32)
    vt_ref[0, 0] = (vt + vt_one_ref[...]).astype(BF16)

    q = jnp.dot(xb, wq_ref[...], preferred_element_type=F32) * (HEAD_DIM ** -0.5 * LOG2E)
    k = jnp.dot(xb, wk_ref[...], preferred_element_type=F32)
    for hb in range(N_HEAD_BLOCKS):
        qq_ref[0, :, 2 * hb * LANES:(2 * hb + 1) * LANES] = q[:, hb * LANES:(hb + 1) * LANES].astype(BF16)
        kk_ref[0, :, 2 * hb * LANES:(2 * hb + 1) * LANES] = k[:, hb * LANES:(hb + 1) * LANES].astype(BF16)


def _in_proj(x, wq, wk, wvt, vt_one, wf, bf, wxg, cw, cb, wri, bri, lam, tm):
    B, S, D = x.shape
    const = lambda b, s: (0, 0)
    tile = lambda b, s: (b, s, 0)
    weights = (wq, wk, wvt, vt_one, wf, bf, wxg, cw, cb, wri, bri, lam)
    return pl.pallas_call(
        _in_proj_kernel,
        grid=(B, S // tm),
        in_specs=[pl.BlockSpec((1, tm, D), tile)] + [pl.BlockSpec(w.shape, const) for w in weights],
        out_specs=[
            pl.BlockSpec((1, tm, 2 * ATT_WIDTH), tile),
            pl.BlockSpec((1, tm, 2 * ATT_WIDTH), tile),
            pl.BlockSpec((1, 1, N_HEADS * VT_ROWS, tm), lambda b, s: (b, s, 0, 0)),
            pl.BlockSpec((1, tm, LRU_WIDTH), tile),
        ],
        out_shape=[
            jax.ShapeDtypeStruct((B, S, 2 * ATT_WIDTH), BF16),
            jax.ShapeDtypeStruct((B, S, 2 * ATT_WIDTH), BF16),
            jax.ShapeDtypeStruct((B, S // tm, N_HEADS * VT_ROWS, tm), BF16),
            jax.ShapeDtypeStruct((B, S, LRU_WIDTH), BF16),
        ],
        scratch_shapes=[
            pltpu.VMEM((1, LANES), F32),
            pltpu.VMEM((tm + SUBLANES, LRU_WIDTH), F32),
            pltpu.VMEM((SUBLANES, LRU_WIDTH), F32),
        ],
        compiler_params=_cparams(("parallel", "arbitrary")),
        name="in_proj_lru",
    )(x, *weights)


def _attn_kernel(qq_ref, kk_ref, vt_ref, o_ref, st0_ref, st1_ref, bm0_ref, bm1_ref, m_ref, acc_ref, *, blk):
    qi = pl.program_id(2)
    st_refs = (st0_ref, st1_ref)
    bm_refs = (bm0_ref, bm1_ref)
    lane = lax.broadcasted_iota(jnp.int32, (blk, 2 * LANES), 1)
    q_heads = []
    for pair in range(ATTN_PAIRS):
        q = qq_ref[0, :, pair * 2 * LANES:(pair + 1) * 2 * LANES]
        zero = jnp.zeros_like(q)
        for h in range(HEADS_PER_BLOCK):
            own = ((lane >= h * HEAD_DIM) & (lane < (h + 1) * HEAD_DIM)) | (
                (lane >= LANES + h * BIAS_SLOT) & (lane < LANES + (h + 1) * BIAS_SLOT))
            q_heads.append(jnp.where(own, q, zero))

    m_ref[...] = jnp.full_like(m_ref, NEG_INF)
    acc_ref[...] = jnp.zeros_like(acc_ref)

    def scores(ki, h, slot, masked):
        pair = h // HEADS_PER_BLOCK
        k_blk = kk_ref[0, pl.ds(pl.multiple_of(ki * blk, blk), blk),
                       pair * 2 * LANES:(pair + 1) * 2 * LANES]
        st = lax.dot_general(k_blk, q_heads[h], (((1,), (1,)), ((), ())),
                             preferred_element_type=F32)
        if masked:
            key = lax.broadcasted_iota(jnp.int32, (blk, blk), 0)
            qry = lax.broadcasted_iota(jnp.int32, (blk, blk), 1)
            st = jnp.where(key - qry <= (qi - ki) * blk, st, NEG_INF)
        st_refs[slot][h] = st
        bm_refs[slot][h] = jnp.max(st, axis=0, keepdims=True)

    def update(ki, h, slot):
        m_old = m_ref[h]
        m_new = jnp.maximum(m_old, bm_refs[slot][h])
        alpha = jnp.exp2(m_old - m_new)
        p = jnp.exp2(st_refs[slot][h] - m_new).astype(BF16)
        vt = vt_ref[0, ki, h * VT_ROWS:(h + 1) * VT_ROWS, :]
        acc_ref[h] = alpha * acc_ref[h] + jnp.dot(vt, p, preferred_element_type=F32)
        m_ref[h] = m_new

    def stage(score_blk, score_slot, masked, update_blk, update_slot):
        for h in range(ATTN_HEADS):
            if score_blk is not None:
                scores(score_blk, h, score_slot, masked)
            if update_blk is not None:
                update(update_blk, h, update_slot)

    stage(0, 0, True, None, None)

    def body(j, carry):
        i = 2 * j
        stage(i + 1, 1, False, i, 0)
        stage(i + 2, 0, False, i + 1, 1)
        return carry

    lax.fori_loop(0, (qi - 1) // 2, body, 0)

    @pl.when(qi == 0)
    def _():
        stage(None, None, False, 0, 0)

    @pl.when(qi % 2 == 1)
    def _():
        stage(qi, 1, True, qi - 1, 0)
        stage(None, None, False, qi, 1)

    @pl.when((qi % 2 == 0) & (qi > 0))
    def _():
        stage(qi - 1, 1, False, qi - 2, 0)
        stage(qi, 0, True, qi - 1, 1)
        stage(None, None, False, qi, 0)

    ot = jnp.concatenate([acc_ref[h, :HEAD_DIM, :] / acc_ref[h, HEAD_DIM:HEAD_DIM + 1, :]
                          for h in range(ATTN_HEADS)], axis=0)
    o_ref[0] = ot.T.astype(o_ref.dtype)


def _attention(qq, kk, vt, blk):
    B, nblk, _, _ = vt.shape
    S = nblk * blk
    return pl.pallas_call(
        functools.partial(_attn_kernel, blk=blk),
        grid=(B, N_HEAD_BLOCKS // ATTN_PAIRS, nblk),
        in_specs=[
            pl.BlockSpec((1, blk, ATTN_PAIRS * 2 * LANES), lambda b, h, i: (b, i, h)),
            pl.BlockSpec((1, S, ATTN_PAIRS * 2 * LANES), lambda b, h, i: (b, 0, h)),
            pl.BlockSpec((1, nblk, ATTN_HEADS * VT_ROWS, blk), lambda b, h, i: (b, 0, h, 0)),
        ],
        out_specs=pl.BlockSpec((1, blk, ATTN_HEADS * HEAD_DIM), lambda b, h, i: (b, i, h)),
        out_shape=jax.ShapeDtypeStruct((B, S, ATT_WIDTH), BF16),
        scratch_shapes=[
            pltpu.VMEM((ATTN_HEADS, blk, blk), F32),
            pltpu.VMEM((ATTN_HEADS, blk, blk), F32),
            pltpu.VMEM((ATTN_HEADS, 1, blk), F32),
            pltpu.VMEM((ATTN_HEADS, 1, blk), F32),
            pltpu.VMEM((ATTN_HEADS, 1, blk), F32),
            pltpu.VMEM((ATTN_HEADS, VT_ROWS, blk), F32),
        ],
        compiler_params=_cparams(("parallel", "parallel", "arbitrary")),
        name="fox_attention",
    )(qq, kk, vt)


def _layer_norm(y, g, b):
    mu = jnp.mean(y, axis=-1, keepdims=True)
    yc = y - mu
    var = jnp.mean(yc * yc, axis=-1, keepdims=True)
    return yc * lax.rsqrt(var + LN_EPS) * g + b


def _route(logits_t):
    mx = jnp.max(logits_t, axis=0, keepdims=True)
    e = jnp.exp(logits_t - mx)
    p = e / jnp.sum(e, axis=0, keepdims=True)
    v = [p[SUBLANES * j:SUBLANES * (j + 1), :] for j in range(EXPERTS_PER_GROUP)]

    def first_max(vals):
        best = functools.reduce(jnp.maximum, vals)
        idx = jnp.full(best.shape, len(vals) - 1, jnp.int32)
        for j in range(len(vals) - 2, -1, -1):
            idx = jnp.where(vals[j] == best, j, idx)
        return best, idx

    m1, idx1 = first_max(v)
    m2, idx2 = first_max([jnp.where(idx1 == j, -1.0, v[j]) for j in range(EXPERTS_PER_GROUP)])
    grp = lax.broadcasted_iota(jnp.int32, m1.shape, 0)
    real = grp < N_GROUPS
    score = jnp.where(real, m1 + m2, -1.0)
    best = jnp.max(score, axis=0, keepdims=True)
    chosen = jnp.min(jnp.where(score == best, grp, SUBLANES), axis=0, keepdims=True)
    sel = grp == chosen
    den = jnp.where(real, m1 + m2, 1.0)
    g1 = m1 / den
    g2 = m2 / den
    comb = [jnp.where(sel, jnp.where(idx1 == j, g1, 0.0) + jnp.where(idx2 == j, g2, 0.0), 0.0)
            for j in range(EXPERTS_PER_GROUP)]
    return comb, chosen


def _sort_by_group(chosen, tri_ref):
    tm = chosen.shape[1]
    grp = lax.broadcasted_iota(jnp.int32, (SUBLANES, tm), 0)
    member = grp == chosen
    onehot = jnp.where(member, 1.0, 0.0).astype(BF16)
    count = jnp.dot(onehot, tri_ref[...], preferred_element_type=F32)
    total = count[:, tm - 1:tm]
    gcol = lax.broadcasted_iota(jnp.int32, (SUBLANES, 1), 0)
    start = jnp.zeros((SUBLANES, 1), F32)
    for k in range(N_GROUPS - 1):
        start = start + jnp.where(gcol > k, total[k:k + 1, :], 0.0)
    pos = jnp.sum(jnp.where(member, start + count - 1.0, 0.0), axis=0, keepdims=True)
    return pos, start, start + total


def _mix_kernel(x_ref, attn_ref, rec_ref, ng_ref, wout_ref, lng_ref, lnb_ref, wr_ref, br_ref, tri_ref,
                x1_ref, x1b_ref, comb_ref, pos_ref, seg_ref, *, alpha):
    tm = x_ref.shape[0]
    chunk = min(MIX_CHUNK, tm)
    rows = ROUTE_ROWS
    nt = (((1,), (1,)), ((), ()))
    comb_parts, chosen_parts = [], []

    def project(r0):
        rs = slice(r0, r0 + chunk)
        ya = attn_ref[rs, :].astype(F32)
        yr = rec_ref[rs, :].astype(F32)
        ya = ya * lax.rsqrt(jnp.mean(ya * ya, axis=-1, keepdims=True) + RMS_EPS) * ng_ref[:, :ATT_WIDTH]
        yr = yr * lax.rsqrt(jnp.mean(yr * yr, axis=-1, keepdims=True) + RMS_EPS) * ng_ref[:, ATT_WIDTH:]
        out = jnp.dot(ya.astype(BF16), wout_ref[:ATT_WIDTH, :], preferred_element_type=F32)
        return out + jnp.dot(yr.astype(BF16), wout_ref[ATT_WIDTH:, :], preferred_element_type=F32)

    def norm_and_route(r0, out):
        rs = slice(r0, r0 + chunk)
        x1 = _layer_norm(alpha * x_ref[rs, :] + out, lng_ref[...], lnb_ref[...])
        x1_ref[rs, :] = x1
        x1_hi = x1.astype(BF16)
        x1b_ref[rs, :] = x1_hi
        x1_lo = (x1 - x1_hi.astype(F32)).astype(BF16)
        o = (lax.dot_general(wr_ref[...], x1_hi, nt, preferred_element_type=F32)
             + lax.dot_general(wr_ref[...], x1_lo, nt, preferred_element_type=F32))
        logits_t = o[:rows, :] + o[rows:, :] + br_ref[...]
        comb_t, chosen = _route(logits_t)
        comb_parts.append(jnp.concatenate(comb_t, axis=0))
        chosen_parts.append(chosen)

    def emit_sort_tile(t):
        per = SORT_TILE // chunk
        chosen = jnp.concatenate(chosen_parts[t * per:(t + 1) * per], axis=1)
        pos, seg_start, seg_end = _sort_by_group(chosen, tri_ref)
        ts_ = slice(t * SORT_TILE, (t + 1) * SORT_TILE)
        pos_ref[0, :, ts_] = pos
        lane = lax.broadcasted_iota(jnp.int32, (SUBLANES, LANES), 1)
        seg_ref[0, t * SUBLANES:(t + 1) * SUBLANES, :] = jnp.where(
            lane == 0, seg_start, jnp.where(lane == 1, seg_end, 0.0))
        meta_row = lax.broadcasted_iota(jnp.int32, (SUBLANES, SORT_TILE), 0)
        meta = jnp.where(meta_row == 0, chosen.astype(F32), jnp.where(meta_row == 1, pos, 0.0))
        comb_t = jnp.concatenate([jnp.concatenate(comb_parts[t * per:(t + 1) * per], axis=1), meta,
                                  jnp.zeros((LANES - rows - SUBLANES, SORT_TILE), F32)], axis=0)
        comb_ref[ts_, :] = comb_t.T

    starts = list(range(0, tm, chunk))
    pending = project(starts[0])
    for k, r0 in enumerate(starts):
        nxt = project(starts[k + 1]) if k + 1 < len(starts) else None
        norm_and_route(r0, pending)
        pending = nxt
        if (r0 + chunk) % SORT_TILE == 0:
            emit_sort_tile(r0 // SORT_TILE)


def _mix_out(x2d, attn2d, rec2d, ng, wout, lng, lnb, wr, br, tri, alpha, tm):
    T, D = x2d.shape
    const = lambda i: (0, 0)
    tile = lambda i: (i, 0)
    n_tiles = T // tm
    sort_tiles = tm // SORT_TILE
    return pl.pallas_call(
        functools.partial(_mix_kernel, alpha=alpha),
        grid=(T // tm,),
        in_specs=[
            pl.BlockSpec((tm, D), tile),
            pl.BlockSpec((tm, ATT_WIDTH), tile),
            pl.BlockSpec((tm, LRU_WIDTH), tile),
            pl.BlockSpec(ng.shape, const),
            pl.BlockSpec(wout.shape, const),
            pl.BlockSpec(lng.shape, const),
            pl.BlockSpec(lnb.shape, const),
            pl.BlockSpec(wr.shape, const),
            pl.BlockSpec(br.shape, const),
            pl.BlockSpec(tri.shape, const),
        ],
        out_specs=[
            pl.BlockSpec((tm, D), tile),
            pl.BlockSpec((tm, D), tile),
            pl.BlockSpec((tm, LANES), tile),
            pl.BlockSpec((1, 1, tm), lambda i: (i, 0, 0)),
            pl.BlockSpec((1, sort_tiles * SUBLANES, LANES), lambda i: (i, 0, 0)),
        ],
        out_shape=[
            jax.ShapeDtypeStruct((T, D), F32),
            jax.ShapeDtypeStruct((T, D), BF16),
            jax.ShapeDtypeStruct((T, LANES), F32),
            jax.ShapeDtypeStruct((n_tiles, 1, tm), F32),
            jax.ShapeDtypeStruct((n_tiles, sort_tiles * SUBLANES, LANES), F32),
        ],
        compiler_params=_cparams(("parallel",)),
        name="mix_out_router",
    )(x2d, attn2d, rec2d, ng, wout, lng, lnb, wr, br, tri)


def _moe_kernel(seg_ref, x1_ref, xb_ref, comb_ref, pos_ref, wgu_ref, wd_ref, lng_ref, lnb_ref, o_ref,
                xs_ref, cs_ref, ys_ref, *, alpha):
    i = pl.program_id(0)
    g = pl.program_id(1)
    tm = xb_ref.shape[0]
    F = wd_ref.shape[1]
    sort_tiles = tm // SORT_TILE

    @pl.when(g == 0)
    def _():
        comb = comb_ref[...]
        hi = comb.astype(BF16)
        lo = (comb - hi.astype(F32)).astype(BF16)
        comb_hl = jnp.concatenate([hi, lo], axis=1)
        for r0 in range(0, tm, PERM_ROWS):
            t0 = (r0 // SORT_TILE) * SORT_TILE
            ts_ = slice(t0, t0 + SORT_TILE)
            pos = pos_ref[0, :, ts_].astype(jnp.int32)
            srow = (r0 - t0) + lax.broadcasted_iota(jnp.int32, (PERM_ROWS, SORT_TILE), 0)
            perm = jnp.where(srow == pos, 1.0, 0.0).astype(BF16)
            xs_ref[r0:r0 + PERM_ROWS, :] = jnp.dot(perm, xb_ref[ts_, :], preferred_element_type=F32).astype(BF16)
            c2 = jnp.dot(perm, comb_hl[ts_, :], preferred_element_type=F32)
            cs_ref[r0:r0 + PERM_ROWS, :] = c2[:, :LANES] + c2[:, LANES:]
        ys_ref[...] = jnp.zeros_like(ys_ref)

    lane = lax.broadcasted_iota(jnp.int32, (sort_tiles * MOE_CHUNK, LANES), 1)
    crow = lax.broadcasted_iota(jnp.int32, (MOE_CHUNK, 1), 0)
    bounds = []
    n_chunks = 0
    for t in range(sort_tiles):
        s_idx = ((i * sort_tiles + t) * N_GROUPS + g) * 2
        start = seg_ref[s_idx]
        end = seg_ref[s_idx + 1]
        base = (start // BF16_SUBLANES) * BF16_SUBLANES
        bounds.append((start, end, base))
        n_chunks = jnp.maximum(n_chunks, (end - base + MOE_CHUNK - 1) // MOE_CHUNK)

    def chunk(c, carry):
        windows, lives = [], []
        for t, (start, end, base) in enumerate(bounds):
            lo_row = base + c * MOE_CHUNK
            local = jnp.minimum(lo_row, SORT_TILE - MOE_CHUNK)
            windows.append(pl.ds(pl.multiple_of(t * SORT_TILE + local, BF16_SUBLANES), MOE_CHUNK))
            lrow = local + crow
            lives.append((lrow >= jnp.maximum(start, lo_row)) & (lrow < jnp.minimum(end, lo_row + MOE_CHUNK)))
        xs = jnp.concatenate([xs_ref[w, :] for w in windows], axis=0)
        cs = jnp.concatenate([cs_ref[w, :] for w in windows], axis=0)
        live = jnp.concatenate(lives, axis=0)
        acc = jnp.zeros((sort_tiles * MOE_CHUNK, o_ref.shape[1]), F32)
        for j in range(EXPERTS_PER_GROUP):
            gu = jnp.dot(xs, wgu_ref[j], preferred_element_type=F32)
            gate = gu[:, :F]
            w = jnp.sum(jnp.where(lane == SUBLANES * j + g, cs, 0.0), axis=1, keepdims=True)
            w = jnp.where(live, w, 0.0)
            act = gate * jax.nn.sigmoid(gate) * gu[:, F:] * w
            acc = acc + jnp.dot(act.astype(BF16), wd_ref[j], preferred_element_type=F32)
        for t, w in enumerate(windows):
            ys_ref[w, :] += acc[t * MOE_CHUNK:(t + 1) * MOE_CHUNK, :]
        return carry

    lax.fori_loop(0, n_chunks, chunk, 0)

    @pl.when(g == N_GROUPS - 1)
    def _():
        scol = lax.broadcasted_iota(jnp.int32, (PERM_ROWS, SORT_TILE), 1)
        for r0 in range(0, tm, PERM_ROWS):
            rs = slice(r0, r0 + PERM_ROWS)
            t0 = (r0 // SORT_TILE) * SORT_TILE
            pos_col = comb_ref[rs, ROUTE_ROWS + 1:ROUTE_ROWS + 2].astype(jnp.int32)
            perm = jnp.where(scol == pos_col, 1.0, 0.0).astype(BF16)
            y = jnp.dot(perm, ys_ref[t0:t0 + SORT_TILE, :].astype(BF16), preferred_element_type=F32)
            o_ref[rs, :] = _layer_norm(alpha * x1_ref[rs, :] + y, lng_ref[...], lnb_ref[...])


def _moe(seg, x1, x1b, comb, pos, wgu, wd, lng, lnb, alpha, tm):
    T, D = x1.shape
    F = wd.shape[1]
    const = lambda i, g, seg: (0, 0)
    tile = lambda i, g, seg: (i, 0)
    return pl.pallas_call(
        functools.partial(_moe_kernel, alpha=alpha),
        grid_spec=pltpu.PrefetchScalarGridSpec(
            num_scalar_prefetch=1,
            grid=(T // tm, N_GROUPS),
            in_specs=[
                pl.BlockSpec((tm, D), tile),
                pl.BlockSpec((tm, D), tile),
                pl.BlockSpec((tm, LANES), tile),
                pl.BlockSpec((1, 1, tm), lambda i, g, seg: (i, 0, 0)),
                pl.BlockSpec((EXPERTS_PER_GROUP, D, 2 * F), lambda i, g, seg: (g, 0, 0)),
                pl.BlockSpec((EXPERTS_PER_GROUP, F, D), lambda i, g, seg: (g, 0, 0)),
                pl.BlockSpec(lng.shape, const),
                pl.BlockSpec(lnb.shape, const),
            ],
            out_specs=pl.BlockSpec((tm, D), tile),
            scratch_shapes=[
                pltpu.VMEM((tm, D), BF16),
                pltpu.VMEM((tm, LANES), F32),
                pltpu.VMEM((tm, D), F32),
            ],
        ),
        out_shape=jax.ShapeDtypeStruct((T, D), F32),
        compiler_params=_cparams(("parallel", "arbitrary")),
        name="moe_ffn",
    )(seg, x1, x1b, comb, pos, wgu, wd, lng, lnb)


def _block_diag(w):
    n, c, d = w.shape
    return jnp.einsum('ncd,nm->ncmd', w, jnp.eye(n, dtype=w.dtype)).reshape(n * c, n * d)


def _router_operands(w_router, b_router):
    D = w_router.shape[0]
    wt = w_router.T.reshape(N_GROUPS, EXPERTS_PER_GROUP, D).transpose(1, 0, 2)
    wt = jnp.pad(wt, ((0, 0), (0, SUBLANES - N_GROUPS), (0, 0))).reshape(4 * SUBLANES, D)
    hi = wt.astype(BF16)
    lo = (wt - hi.astype(F32)).astype(BF16)
    bt = b_router.reshape(N_GROUPS, EXPERTS_PER_GROUP).T
    bt = jnp.pad(bt, ((0, 0), (0, SUBLANES - N_GROUPS)), constant_values=NEG_INF).reshape(4 * SUBLANES, 1)
    return jnp.concatenate([hi, lo], axis=0), bt.astype(F32)


def kernel(x, w_in, b_forget, conv_w, conv_b, w_rec_gate, b_rec_gate, w_in_gate, b_in_gate, lru_lambda,
           mix_norm_g, w_out, ln1_g, ln1_b, w_router, b_router, w_exp_gate, w_exp_up, w_exp_down,
           ln2_g, ln2_b):
    B, S, D = x.shape
    depth = w_in.shape[0]
    alpha = (2.0 * depth) ** 0.25
    T = B * S
    o_f = 3 * ATT_WIDTH
    o_x = o_f + N_HEADS

    blk_attn = min(512, S)
    tm_moe = min(1024, T)
    tri = jnp.triu(jnp.ones((SORT_TILE, SORT_TILE), F32)).astype(BF16)

    wr, br = _router_operands(w_router, b_router)
    row = lambda a: a.reshape(1, -1).astype(F32)
    vt_one = (jnp.arange(N_HEADS * VT_ROWS) % VT_ROWS == HEAD_DIM).astype(F32).reshape(-1, 1)

    for l in range(depth):
        wq = w_in[l, :, :ATT_WIDTH].astype(BF16)
        wk = w_in[l, :, ATT_WIDTH:2 * ATT_WIDTH].astype(BF16)
        wvt = w_in[l, :, 2 * ATT_WIDTH:o_f].T.reshape(N_HEADS, HEAD_DIM, D)
        wvt = jnp.pad(wvt, ((0, 0), (0, VT_ROWS - HEAD_DIM), (0, 0))).reshape(N_HEADS * VT_ROWS, D).astype(BF16)
        wf = jnp.pad(w_in[l, :, o_f:o_x], ((0, 0), (0, LANES - N_HEADS))).astype(BF16)
        bf = jnp.pad(b_forget[l], (0, LANES - N_HEADS)).reshape(1, LANES).astype(F32)
        wxg = w_in[l, :, o_x:].astype(BF16)
        wri = jnp.concatenate([_block_diag(w_rec_gate[l]), _block_diag(w_in_gate[l])], axis=1).astype(BF16)
        bri = jnp.concatenate([b_rec_gate[l], b_in_gate[l]]).reshape(1, -1).astype(F32)
        qq, kk, vt, rec = _in_proj(x, wq, wk, wvt, vt_one, wf, bf, wxg, conv_w[l].astype(F32), row(conv_b[l]),
                                   wri, bri, row(lru_lambda[l]), blk_attn)

        attn = _attention(qq, kk, vt, blk_attn)

        x1, x1b, comb, pos, seg = _mix_out(
            x.reshape(T, D), attn.reshape(T, ATT_WIDTH), rec.reshape(T, LRU_WIDTH), row(mix_norm_g[l]),
            w_out[l].astype(BF16), row(ln1_g[l]), row(ln1_b[l]), wr, br, tri, alpha, tm_moe)

        seg = seg.reshape(-1, SUBLANES, LANES)[:, :N_GROUPS, :2].astype(jnp.int32).reshape(-1)
        wgu = jnp.concatenate([w_exp_gate[l], w_exp_up[l]], axis=-1).astype(BF16)
        x = _moe(seg, x1, x1b, comb, pos, wgu, w_exp_down[l].astype(BF16), row(ln2_g[l]), row(ln2_b[l]),
                 alpha, tm_moe).reshape(B, S, D).astype(x.dtype)
    return x
```

```python
import functools

import jax
import jax.numpy as jnp
from jax import lax
from jax.experimental import pallas as pl
from jax.experimental.pallas import tpu as pltpu

F32 = jnp.float32
BF16 = jnp.bfloat16

HEAD_DIM = 64
N_HEADS = 8
ATT_WIDTH = N_HEADS * HEAD_DIM
LRU_WIDTH = 512
LRU_C = 8.0
CONV_W = 4
N_GROUPS = 4
EXPERTS_PER_GROUP = 4
N_EXPERTS = N_GROUPS * EXPERTS_PER_GROUP
LN_EPS = 1e-5
RMS_EPS = 1e-6
NEG_INF = -1e30

LANES = 128
SUBLANES = 8
HEADS_PER_BLOCK = LANES // HEAD_DIM
N_HEAD_BLOCKS = N_HEADS // HEADS_PER_BLOCK
BIAS_SLOT = 8
ATTN_PAIRS = 4
ATTN_HEADS = ATTN_PAIRS * HEADS_PER_BLOCK
BF16_SUBLANES = 16
VT_ROWS = HEAD_DIM + BF16_SUBLANES
LOG2E = 1.4426950408889634
GELU_K = 0.7978845608028654
VMEM_LIMIT = 56 * 1024 * 1024
MIX_CHUNK = 256
SORT_TILE = 512
MOE_CHUNK = 160
PERM_ROWS = 256
ROUTE_ROWS = 4 * SUBLANES


def _cparams(sem):
    return pltpu.CompilerParams(dimension_semantics=sem, vmem_limit_bytes=VMEM_LIMIT)


def _lru_tile(x, gate, cw_ref, cb_ref, wri_ref, bri_ref, lam_ref, ext_ref, h_ref):
    ts, W = x.shape
    ext_ref[SUBLANES:SUBLANES + ts, :] = x
    xc = cb_ref[...] + cw_ref[CONV_W - 1:CONV_W, :] * x
    for back in range(1, CONV_W):
        xc = xc + cw_ref[CONV_W - 1 - back:CONV_W - back, :] * ext_ref[SUBLANES - back:SUBLANES - back + ts, :]
    ext_ref[0:SUBLANES, :] = x[ts - SUBLANES:ts, :]

    ri = jnp.dot(xc.astype(BF16), wri_ref[...], preferred_element_type=F32) + bri_ref[...]
    r = 0.5 * jnp.tanh(0.5 * ri[:, :W]) + 0.5
    i = 0.5 * jnp.tanh(0.5 * ri[:, W:]) + 0.5
    neg_lam = -lam_ref[...]
    softplus = jnp.maximum(neg_lam, 0.0) + jnp.log1p(jnp.exp(-jnp.abs(neg_lam)))
    a = jnp.exp(-LRU_C * r * softplus)
    v = jnp.maximum(1.0 - a * a, 0.0)
    u = jnp.where(v > 0.0, v * lax.rsqrt(v), 0.0) * (i * xc)

    row = lax.broadcasted_iota(jnp.int32, (SUBLANES, W), 0)
    h_prev = h_ref[...]
    hs = []
    for j in range(ts // SUBLANES):
        A = a[j * SUBLANES:(j + 1) * SUBLANES, :]
        U = u[j * SUBLANES:(j + 1) * SUBLANES, :]
        for d in (1, 2, 4):
            keep = row >= d
            U = U + A * jnp.where(keep, pltpu.roll(U, d, axis=0), 0.0)
            A = A * jnp.where(keep, pltpu.roll(A, d, axis=0), 1.0)
        H = A * h_prev + U
        hs.append(H)
        h_prev = jnp.broadcast_to(H[SUBLANES - 1:SUBLANES, :], (SUBLANES, W))
    h_ref[...] = h_prev
    h = jnp.concatenate(hs, axis=0)

    inner = gate * (GELU_K + (GELU_K * 0.044715) * (gate * gate))
    half = 0.5 * gate
    return (half + half * jnp.tanh(inner)) * h


def _in_proj_kernel(x_ref, wq_ref, wk_ref, wvt_ref, vt_one_ref, wf_ref, bf_ref, wxg_ref,
                    cw_ref, cb_ref, wri_ref, bri_ref, lam_ref,
                    qq_ref, kk_ref, vt_ref, rec_ref, carry_ref, ext_ref, h_ref):
    tm = x_ref.shape[1]

    @pl.when(pl.program_id(1) == 0)
    def _():
        carry_ref[...] = jnp.zeros_like(carry_ref)
        ext_ref[0:SUBLANES, :] = jnp.zeros((SUBLANES, LRU_WIDTH), F32)
        h_ref[...] = jnp.zeros_like(h_ref)

    xb = x_ref[0].astype(BF16)
    xg = jnp.dot(xb, wxg_ref[...], preferred_element_type=F32)
    rec = _lru_tile(xg[:, :LRU_WIDTH], xg[:, LRU_WIDTH:], cw_ref, cb_ref, wri_ref, bri_ref, lam_ref,
                    ext_ref, h_ref)
    rec_ref[0] = rec.astype(rec_ref.dtype)

    f = jnp.dot(xb, wf_ref[...], preferred_element_type=F32) + bf_ref[...]

    c = jnp.minimum(f, 0.0) - jnp.log1p(jnp.exp(-jnp.abs(f)))
    row = lax.broadcasted_iota(jnp.int32, c.shape, 0)
    d = 1
    while d < tm:
        c = c + jnp.where(row >= d, pltpu.roll(c, d, axis=0), 0.0)
        d *= 2
    c = c + carry_ref[...]
    carry_ref[...] = c[tm - 1:tm, :]

    c2 = c * LOG2E
    parts = []
    for h in range(N_HEADS):
        col = c2[:, h:h + 1]
        hi = col.astype(BF16).astype(F32)
        rem = col - hi
        mid = rem.astype(BF16).astype(F32)
        parts.append((hi, mid, rem - mid))

    lane = lax.broadcasted_iota(jnp.int32, (tm, LANES), 1)
    slot = lane % BIAS_SLOT
    head_slot = lane // BIAS_SLOT
    valid = head_slot < HEADS_PER_BLOCK
    for hb in range(N_HEAD_BLOCKS):
        def part(k):
            return jnp.where(head_slot == 0, parts[2 * hb][k], parts[2 * hb + 1][k])

        cq = jnp.where(slot == 0, part(0), jnp.where(slot == 1, part(1), part(2)))
        ck = jnp.where(slot == 3, part(0), jnp.where(slot == 4, part(1), part(2)))
        qx = jnp.where(valid & (slot < 3), cq, jnp.where(valid & (slot < 6), 1.0, 0.0))
        kx = jnp.where(valid & (slot < 3), 1.0, jnp.where(valid & (slot < 6), -ck, 0.0))
        qq_ref[0, :, (2 * hb + 1) * LANES:(2 * hb + 2) * LANES] = qx.astype(BF16)
        kk_ref[0, :, (2 * hb + 1) * LANES:(2 * hb + 2) * LANES] = kx.astype(BF16)

    vt = lax.dot_general(wvt_ref[...], xb, (((1,), (1,)), ((), ())), preferred_element_type=F32)
    vt_ref[0, 0] = (vt + vt_one_ref[...]).astype(BF16)

    q = jnp.dot(xb, wq_ref[...], preferred_element_type=F32) * (HEAD_DIM ** -0.5 * LOG2E)
    k = jnp.dot(xb, wk_ref[...], preferred_element_type=F32)
    for hb in range(N_HEAD_BLOCKS):
        qq_ref[0, :, 2 * hb * LANES:(2 * hb + 1) * LANES] = q[:, hb * LANES:(hb + 1) * LANES].astype(BF16)
        kk_ref[0, :, 2 * hb * LANES:(2 * hb + 1) * LANES] = k[:, hb * LANES:(hb + 1) * LANES].astype(BF16)


def _in_proj(x, wq, wk, wvt, vt_one, wf, bf, wxg, cw, cb, wri, bri, lam, tm):
    B, S, D = x.shape
    const = lambda b, s: (0, 0)
    tile = lambda b, s: (b, s, 0)
    weights = (wq, wk, wvt, vt_one, wf, bf, wxg, cw, cb, wri, bri, lam)
    return pl.pallas_call(
        _in_proj_kernel,
        grid=(B, S // tm),
        in_specs=[pl.BlockSpec((1, tm, D), tile)] + [pl.BlockSpec(w.shape, const) for w in weights],
        out_specs=[
            pl.BlockSpec((1, tm, 2 * ATT_WIDTH), tile),
            pl.BlockSpec((1, tm, 2 * ATT_WIDTH), tile),
            pl.BlockSpec((1, 1, N_HEADS * VT_ROWS, tm), lambda b, s: (b, s, 0, 0)),
            pl.BlockSpec((1, tm, LRU_WIDTH), tile),
        ],
        out_shape=[
            jax.ShapeDtypeStruct((B, S, 2 * ATT_WIDTH), BF16),
            jax.ShapeDtypeStruct((B, S, 2 * ATT_WIDTH), BF16),
            jax.ShapeDtypeStruct((B, S // tm, N_HEADS * VT_ROWS, tm), BF16),
            jax.ShapeDtypeStruct((B, S, LRU_WIDTH), BF16),
        ],
        scratch_shapes=[
            pltpu.VMEM((1, LANES), F32),
            pltpu.VMEM((tm + SUBLANES, LRU_WIDTH), F32),
            pltpu.VMEM((SUBLANES, LRU_WIDTH), F32),
        ],
        compiler_params=_cparams(("parallel", "arbitrary")),
        name="in_proj_lru",
    )(x, *weights)


def _attn_kernel(qq_ref, kk_ref, vt_ref, o_ref, st0_ref, st1_ref, bm0_ref, bm1_ref, m_ref, acc_ref, *, blk):
    qi = pl.program_id(2)
    st_refs = (st0_ref, st1_ref)
    bm_refs = (bm0_ref, bm1_ref)
    lane = lax.broadcasted_iota(jnp.int32, (blk, 2 * LANES), 1)
    q_heads = []
    for pair in range(ATTN_PAIRS):
        q = qq_ref[0, :, pair * 2 * LANES:(pair + 1) * 2 * LANES]
        zero = jnp.zeros_like(q)
        for h in range(HEADS_PER_BLOCK):
            own = ((lane >= h * HEAD_DIM) & (lane < (h + 1) * HEAD_DIM)) | (
                (lane >= LANES + h * BIAS_SLOT) & (lane < LANES + (h + 1) * BIAS_SLOT))
            q_heads.append(jnp.where(own, q, zero))

    m_ref[...] = jnp.full_like(m_ref, NEG_INF)
    acc_ref[...] = jnp.zeros_like(acc_ref)

    def scores(ki, h, slot, masked):
        pair = h // HEADS_PER_BLOCK
        k_blk = kk_ref[0, pl.ds(pl.multiple_of(ki * blk, blk), blk),
                       pair * 2 * LANES:(pair + 1) * 2 * LANES]
        nt = (((1,), (1,)), ((), ()))
        if masked == "diagonal":
            half = blk // 2
            top = lax.dot_general(k_blk[:half], q_heads[h], nt, preferred_element_type=F32)
            bot = lax.dot_general(k_blk[half:], q_heads[h][half:], nt, preferred_element_type=F32)
            key_t = lax.broadcasted_iota(jnp.int32, (half, blk), 0)
            qry_t = lax.broadcasted_iota(jnp.int32, (half, blk), 1)
            key_b = lax.broadcasted_iota(jnp.int32, (half, half), 0)
            qry_b = lax.broadcasted_iota(jnp.int32, (half, half), 1)
            top = jnp.where(key_t <= qry_t, top, NEG_INF)
            bot = jnp.where(key_b <= qry_b, bot, NEG_INF)
            hidden = jnp.full((half, half), NEG_INF, F32)
            st = jnp.concatenate([top, jnp.concatenate([hidden, bot], axis=1)], axis=0)
        else:
            st = lax.dot_general(k_blk, q_heads[h], nt, preferred_element_type=F32)
            if masked:
                key = lax.broadcasted_iota(jnp.int32, (blk, blk), 0)
                qry = lax.broadcasted_iota(jnp.int32, (blk, blk), 1)
                st = jnp.where(key - qry <= (qi - ki) * blk, st, NEG_INF)
        st_refs[slot][h] = st
        bm_refs[slot][h] = jnp.max(st, axis=0, keepdims=True)

    def update(ki, h, slot):
        m_old = m_ref[h]
        m_new = jnp.maximum(m_old, bm_refs[slot][h])
        alpha = jnp.exp2(m_old - m_new)
        p = jnp.exp2(st_refs[slot][h] - m_new).astype(BF16)
        vt = vt_ref[0, ki, h * VT_ROWS:(h + 1) * VT_ROWS, :]
        acc_ref[h] = alpha * acc_ref[h] + jnp.dot(vt, p, preferred_element_type=F32)
        m_ref[h] = m_new

    def stage(score_blk, score_slot, masked, update_blk, update_slot):
        for h in range(ATTN_HEADS):
            if score_blk is not None:
                scores(score_blk, h, score_slot, masked)
            if update_blk is not None:
                update(update_blk, h, update_slot)

    stage(0, 0, True, None, None)

    def body(j, carry):
        i = 2 * j
        stage(i + 1, 1, False, i, 0)
        stage(i + 2, 0, False, i + 1, 1)
        return carry

    lax.fori_loop(0, (qi - 1) // 2, body, 0)

    @pl.when(qi == 0)
    def _():
        stage(None, None, False, 0, 0)

    @pl.when(qi % 2 == 1)
    def _():
        stage(qi, 1, "diagonal", qi - 1, 0)
        stage(None, None, False, qi, 1)

    @pl.when((qi % 2 == 0) & (qi > 0))
    def _():
        stage(qi - 1, 1, False, qi - 2, 0)
        stage(qi, 0, "diagonal", qi - 1, 1)
        stage(None, None, False, qi, 0)

    ot = jnp.concatenate([acc_ref[h, :HEAD_DIM, :] / acc_ref[h, HEAD_DIM:HEAD_DIM + 1, :]
                          for h in range(ATTN_HEADS)], axis=0)
    o_ref[0] = ot.T.astype(o_ref.dtype)


def _attention(qq, kk, vt, blk):
    B, nblk, _, _ = vt.shape
    S = nblk * blk
    return pl.pallas_call(
        functools.partial(_attn_kernel, blk=blk),
        grid=(B, N_HEAD_BLOCKS // ATTN_PAIRS, nblk),
        in_specs=[
            pl.BlockSpec((1, blk, ATTN_PAIRS * 2 * LANES), lambda b, h, i: (b, i, h)),
            pl.BlockSpec((1, S, ATTN_PAIRS * 2 * LANES), lambda b, h, i: (b, 0, h)),
            pl.BlockSpec((1, nblk, ATTN_HEADS * VT_ROWS, blk), lambda b, h, i: (b, 0, h, 0)),
        ],
        out_specs=pl.BlockSpec((1, blk, ATTN_HEADS * HEAD_DIM), lambda b, h, i: (b, i, h)),
        out_shape=jax.ShapeDtypeStruct((B, S, ATT_WIDTH), BF16),
        scratch_shapes=[
            pltpu.VMEM((ATTN_HEADS, blk, blk), F32),
            pltpu.VMEM((ATTN_HEADS, blk, blk), F32),
            pltpu.VMEM((ATTN_HEADS, 1, blk), F32),
            pltpu.VMEM((ATTN_HEADS, 1, blk), F32),
            pltpu.VMEM((ATTN_HEADS, 1, blk), F32),
            pltpu.VMEM((ATTN_HEADS, VT_ROWS, blk), F32),
        ],
        compiler_params=_cparams(("parallel", "parallel", "arbitrary")),
        name="fox_attention",
    )(qq, kk, vt)


def _layer_norm(y, g, b):
    mu = jnp.mean(y, axis=-1, keepdims=True)
    yc = y - mu
    var = jnp.mean(yc * yc, axis=-1, keepdims=True)
    return yc * lax.rsqrt(var + LN_EPS) * g + b


def _route(logits_t):
    mx = jnp.max(logits_t, axis=0, keepdims=True)
    e = jnp.exp(logits_t - mx)
    p = e / jnp.sum(e, axis=0, keepdims=True)
    v = [p[SUBLANES * j:SUBLANES * (j + 1), :] for j in range(EXPERTS_PER_GROUP)]

    def first_max(vals):
        best = functools.reduce(jnp.maximum, vals)
        idx = jnp.full(best.shape, len(vals) - 1, jnp.int32)
        for j in range(len(vals) - 2, -1, -1):
            idx = jnp.where(vals[j] == best, j, idx)
        return best, idx

    m1, idx1 = first_max(v)
    m2, idx2 = first_max([jnp.where(idx1 == j, -1.0, v[j]) for j in range(EXPERTS_PER_GROUP)])
    grp = lax.broadcasted_iota(jnp.int32, m1.shape, 0)
    real = grp < N_GROUPS
    score = jnp.where(real, m1 + m2, -1.0)
    best = jnp.max(score, axis=0, keepdims=True)
    chosen = jnp.min(jnp.where(score == best, grp, SUBLANES), axis=0, keepdims=True)
    sel = grp == chosen
    den = jnp.where(real, m1 + m2, 1.0)
    g1 = m1 / den
    g2 = m2 / den
    comb = [jnp.where(sel, jnp.where(idx1 == j, g1, 0.0) + jnp.where(idx2 == j, g2, 0.0), 0.0)
            for j in range(EXPERTS_PER_GROUP)]
    return comb, chosen


def _sort_by_group(chosen, tri_ref):
    tm = chosen.shape[1]
    grp = lax.broadcasted_iota(jnp.int32, (SUBLANES, tm), 0)
    member = grp == chosen
    onehot = jnp.where(member, 1.0, 0.0).astype(BF16)
    count = jnp.dot(onehot, tri_ref[...], preferred_element_type=F32)
    total = count[:, tm - 1:tm]
    gcol = lax.broadcasted_iota(jnp.int32, (SUBLANES, 1), 0)
    start = jnp.zeros((SUBLANES, 1), F32)
    for k in range(N_GROUPS - 1):
        start = start + jnp.where(gcol > k, total[k:k + 1, :], 0.0)
    pos = jnp.sum(jnp.where(member, start + count - 1.0, 0.0), axis=0, keepdims=True)
    return pos, start, start + total


def _mix_kernel(x_ref, attn_ref, rec_ref, ng_ref, wout_ref, lng_ref, lnb_ref, wr_ref, br_ref, tri_ref,
                x1_ref, x1b_ref, comb_ref, pos_ref, seg_ref, *, alpha):
    tm = x_ref.shape[0]
    chunk = min(MIX_CHUNK, tm)
    rows = ROUTE_ROWS
    nt = (((1,), (1,)), ((), ()))
    comb_parts, chosen_parts = [], []

    def project(r0):
        rs = slice(r0, r0 + chunk)
        ya = attn_ref[rs, :].astype(F32)
        yr = rec_ref[rs, :].astype(F32)
        ya = ya * lax.rsqrt(jnp.mean(ya * ya, axis=-1, keepdims=True) + RMS_EPS) * ng_ref[:, :ATT_WIDTH]
        yr = yr * lax.rsqrt(jnp.mean(yr * yr, axis=-1, keepdims=True) + RMS_EPS) * ng_ref[:, ATT_WIDTH:]
        out = jnp.dot(ya.astype(BF16), wout_ref[:ATT_WIDTH, :], preferred_element_type=F32)
        return out + jnp.dot(yr.astype(BF16), wout_ref[ATT_WIDTH:, :], preferred_element_type=F32)

    def norm_and_route(r0, out):
        rs = slice(r0, r0 + chunk)
        x1 = _layer_norm(alpha * x_ref[rs, :] + out, lng_ref[...], lnb_ref[...])
        x1_ref[rs, :] = x1
        x1_hi = x1.astype(BF16)
        x1b_ref[rs, :] = x1_hi
        x1_lo = (x1 - x1_hi.astype(F32)).astype(BF16)
        o = (lax.dot_general(wr_ref[...], x1_hi, nt, preferred_element_type=F32)
             + lax.dot_general(wr_ref[...], x1_lo, nt, preferred_element_type=F32))
        logits_t = o[:rows, :] + o[rows:, :] + br_ref[...]
        comb_t, chosen = _route(logits_t)
        comb_parts.append(jnp.concatenate(comb_t, axis=0))
        chosen_parts.append(chosen)

    def emit_sort_tile(t):
        per = SORT_TILE // chunk
        chosen = jnp.concatenate(chosen_parts[t * per:(t + 1) * per], axis=1)
        pos, seg_start, seg_end = _sort_by_group(chosen, tri_ref)
        ts_ = slice(t * SORT_TILE, (t + 1) * SORT_TILE)
        pos_ref[0, :, ts_] = pos
        lane = lax.broadcasted_iota(jnp.int32, (SUBLANES, LANES), 1)
        seg_ref[0, t * SUBLANES:(t + 1) * SUBLANES, :] = jnp.where(
            lane == 0, seg_start, jnp.where(lane == 1, seg_end, 0.0))
        meta_row = lax.broadcasted_iota(jnp.int32, (SUBLANES, SORT_TILE), 0)
        meta = jnp.where(meta_row == 0, chosen.astype(F32), jnp.where(meta_row == 1, pos, 0.0))
        comb_t = jnp.concatenate([jnp.concatenate(comb_parts[t * per:(t + 1) * per], axis=1), meta,
                                  jnp.zeros((LANES - rows - SUBLANES, SORT_TILE), F32)], axis=0)
        comb_ref[ts_, :] = comb_t.T

    starts = list(range(0, tm, chunk))
    pending = project(starts[0])
    for k, r0 in enumerate(starts):
        nxt = project(starts[k + 1]) if k + 1 < len(starts) else None
        norm_and_route(r0, pending)
        pending = nxt
        if (r0 + chunk) % SORT_TILE == 0:
            emit_sort_tile(r0 // SORT_TILE)


def _mix_out(x2d, attn2d, rec2d, ng, wout, lng, lnb, wr, br, tri, alpha, tm):
    T, D = x2d.shape
    const = lambda i: (0, 0)
    tile = lambda i: (i, 0)
    n_tiles = T // tm
    sort_tiles = tm // SORT_TILE
    return pl.pallas_call(
        functools.partial(_mix_kernel, alpha=alpha),
        grid=(T // tm,),
        in_specs=[
            pl.BlockSpec((tm, D), tile),
            pl.BlockSpec((tm, ATT_WIDTH), tile),
            pl.BlockSpec((tm, LRU_WIDTH), tile),
            pl.BlockSpec(ng.shape, const),
            pl.BlockSpec(wout.shape, const),
            pl.BlockSpec(lng.shape, const),
            pl.BlockSpec(lnb.shape, const),
            pl.BlockSpec(wr.shape, const),
            pl.BlockSpec(br.shape, const),
            pl.BlockSpec(tri.shape, const),
        ],
        out_specs=[
            pl.BlockSpec((tm, D), tile),
            pl.BlockSpec((tm, D), tile),
            pl.BlockSpec((tm, LANES), tile),
            pl.BlockSpec((1, 1, tm), lambda i: (i, 0, 0)),
            pl.BlockSpec((1, sort_tiles * SUBLANES, LANES), lambda i: (i, 0, 0)),
        ],
        out_shape=[
            jax.ShapeDtypeStruct((T, D), F32),
            jax.ShapeDtypeStruct((T, D), BF16),
            jax.ShapeDtypeStruct((T, LANES), F32),
            jax.ShapeDtypeStruct((n_tiles, 1, tm), F32),
            jax.ShapeDtypeStruct((n_tiles, sort_tiles * SUBLANES, LANES), F32),
        ],
        compiler_params=_cparams(("parallel",)),
        name="mix_out_router",
    )(x2d, attn2d, rec2d, ng, wout, lng, lnb, wr, br, tri)


def _moe_kernel(seg_ref, x1_ref, xb_ref, comb_ref, pos_ref, wgu_ref, wd_ref, lng_ref, lnb_ref, o_ref,
                xs_ref, cs_ref, ys_ref, *, alpha):
    i = pl.program_id(0)
    g = pl.program_id(1)
    tm = xb_ref.shape[0]
    F = wd_ref.shape[1]
    sort_tiles = tm // SORT_TILE

    @pl.when(g == 0)
    def _():
        comb = comb_ref[...]
        hi = comb.astype(BF16)
        lo = (comb - hi.astype(F32)).astype(BF16)
        comb_hl = jnp.concatenate([hi, lo], axis=1)
        for r0 in range(0, tm, PERM_ROWS):
            t0 = (r0 // SORT_TILE) * SORT_TILE
            ts_ = slice(t0, t0 + SORT_TILE)
            pos = pos_ref[0, :, ts_].astype(jnp.int32)
            srow = (r0 - t0) + lax.broadcasted_iota(jnp.int32, (PERM_ROWS, SORT_TILE), 0)
            perm = jnp.where(srow == pos, 1.0, 0.0).astype(BF16)
            xs_ref[r0:r0 + PERM_ROWS, :] = jnp.dot(perm, xb_ref[ts_, :], preferred_element_type=F32).astype(BF16)
            c2 = jnp.dot(perm, comb_hl[ts_, :], preferred_element_type=F32)
            cs_ref[r0:r0 + PERM_ROWS, :] = c2[:, :LANES] + c2[:, LANES:]
        ys_ref[...] = jnp.zeros_like(ys_ref)

    lane = lax.broadcasted_iota(jnp.int32, (sort_tiles * MOE_CHUNK, LANES), 1)
    crow = lax.broadcasted_iota(jnp.int32, (MOE_CHUNK, 1), 0)
    bounds = []
    n_chunks = 0
    for t in range(sort_tiles):
        s_idx = ((i * sort_tiles + t) * N_GROUPS + g) * 2
        start = seg_ref[s_idx]
        end = seg_ref[s_idx + 1]
        base = (start // BF16_SUBLANES) * BF16_SUBLANES
        bounds.append((start, end, base))
        n_chunks = jnp.maximum(n_chunks, (end - base + MOE_CHUNK - 1) // MOE_CHUNK)

    def chunk(c, carry):
        windows, lives = [], []
        for t, (start, end, base) in enumerate(bounds):
            lo_row = base + c * MOE_CHUNK
            local = jnp.minimum(lo_row, SORT_TILE - MOE_CHUNK)
            windows.append(pl.ds(pl.multiple_of(t * SORT_TILE + local, BF16_SUBLANES), MOE_CHUNK))
            lrow = local + crow
            lives.append((lrow >= jnp.maximum(start, lo_row)) & (lrow < jnp.minimum(end, lo_row + MOE_CHUNK)))
        xs = jnp.concatenate([xs_ref[w, :] for w in windows], axis=0)
        cs = jnp.concatenate([cs_ref[w, :] for w in windows], axis=0)
        live = jnp.concatenate(lives, axis=0)
        acc = jnp.zeros((sort_tiles * MOE_CHUNK, o_ref.shape[1]), F32)
        for j in range(EXPERTS_PER_GROUP):
            gu = jnp.dot(xs, wgu_ref[j], preferred_element_type=F32)
            gate = gu[:, :F]
            w = jnp.sum(jnp.where(lane == SUBLANES * j + g, cs, 0.0), axis=1, keepdims=True)
            w = jnp.where(live, w, 0.0)
            act = gate * jax.nn.sigmoid(gate) * gu[:, F:] * w
            acc = acc + jnp.dot(act.astype(BF16), wd_ref[j], preferred_element_type=F32)
        for t, w in enumerate(windows):
            ys_ref[w, :] += acc[t * MOE_CHUNK:(t + 1) * MOE_CHUNK, :]
        return carry

    lax.fori_loop(0, n_chunks, chunk, 0)

    @pl.when(g == N_GROUPS - 1)
    def _():
        scol = lax.broadcasted_iota(jnp.int32, (PERM_ROWS, SORT_TILE), 1)
        for r0 in range(0, tm, PERM_ROWS):
            rs = slice(r0, r0 + PERM_ROWS)
            t0 = (r0 // SORT_TILE) * SORT_TILE
            pos_col = comb_ref[rs, ROUTE_ROWS + 1:ROUTE_ROWS + 2].astype(jnp.int32)
            perm = jnp.where(scol == pos_col, 1.0, 0.0).astype(BF16)
            y = jnp.dot(perm, ys_ref[t0:t0 + SORT_TILE, :].astype(BF16), preferred_element_type=F32)
            o_ref[rs, :] = _layer_norm(alpha * x1_ref[rs, :] + y, lng_ref[...], lnb_ref[...])


def _moe(seg, x1, x1b, comb, pos, wgu, wd, lng, lnb, alpha, tm):
    T, D = x1.shape
    F = wd.shape[1]
    const = lambda i, g, seg: (0, 0)
    tile = lambda i, g, seg: (i, 0)
    return pl.pallas_call(
        functools.partial(_moe_kernel, alpha=alpha),
        grid_spec=pltpu.PrefetchScalarGridSpec(
            num_scalar_prefetch=1,
            grid=(T // tm, N_GROUPS),
            in_specs=[
                pl.BlockSpec((tm, D), tile),
                pl.BlockSpec((tm, D), tile),
                pl.BlockSpec((tm, LANES), tile),
                pl.BlockSpec((1, 1, tm), lambda i, g, seg: (i, 0, 0)),
                pl.BlockSpec((EXPERTS_PER_GROUP, D, 2 * F), lambda i, g, seg: (g, 0, 0)),
                pl.BlockSpec((EXPERTS_PER_GROUP, F, D), lambda i, g, seg: (g, 0, 0)),
                pl.BlockSpec(lng.shape, const),
                pl.BlockSpec(lnb.shape, const),
            ],
            out_specs=pl.BlockSpec((tm, D), tile),
            scratch_shapes=[
                pltpu.VMEM((tm, D), BF16),
                pltpu.VMEM((tm, LANES), F32),
                pltpu.VMEM((tm, D), F32),
            ],
        ),
        out_shape=jax.ShapeDtypeStruct((T, D), F32),
        compiler_params=_cparams(("parallel", "arbitrary")),
        name="moe_ffn",
    )(seg, x1, x1b, comb, pos, wgu, wd, lng, lnb)


def _block_diag(w):
    n, c, d = w.shape
    return jnp.einsum('ncd,nm->ncmd', w, jnp.eye(n, dtype=w.dtype)).reshape(n * c, n * d)


def _router_operands(w_router, b_router):
    D = w_router.shape[0]
    wt = w_router.T.reshape(N_GROUPS, EXPERTS_PER_GROUP, D).transpose(1, 0, 2)
    wt = jnp.pad(wt, ((0, 0), (0, SUBLANES - N_GROUPS), (0, 0))).reshape(4 * SUBLANES, D)
    hi = wt.astype(BF16)
    lo = (wt - hi.astype(F32)).astype(BF16)
    bt = b_router.reshape(N_GROUPS, EXPERTS_PER_GROUP).T
    bt = jnp.pad(bt, ((0, 0), (0, SUBLANES - N_GROUPS)), constant_values=NEG_INF).reshape(4 * SUBLANES, 1)
    return jnp.concatenate([hi, lo], axis=0), bt.astype(F32)


def kernel(x, w_in, b_forget, conv_w, conv_b, w_rec_gate, b_rec_gate, w_in_gate, b_in_gate, lru_lambda,
           mix_norm_g, w_out, ln1_g, ln1_b, w_router, b_router, w_exp_gate, w_exp_up, w_exp_down,
           ln2_g, ln2_b):
    B, S, D = x.shape
    depth = w_in.shape[0]
    alpha = (2.0 * depth) ** 0.25
    T = B * S
    o_f = 3 * ATT_WIDTH
    o_x = o_f + N_HEADS

    blk_attn = min(512, S)
    tm_moe = min(1024, T)
    tri = jnp.triu(jnp.ones((SORT_TILE, SORT_TILE), F32)).astype(BF16)

    wr, br = _router_operands(w_router, b_router)
    row = lambda a: a.reshape(1, -1).astype(F32)
    vt_one = (jnp.arange(N_HEADS * VT_ROWS) % VT_ROWS == HEAD_DIM).astype(F32).reshape(-1, 1)

    for l in range(depth):
        wq = w_in[l, :, :ATT_WIDTH].astype(BF16)
        wk = w_in[l, :, ATT_WIDTH:2 * ATT_WIDTH].astype(BF16)
        wvt = w_in[l, :, 2 * ATT_WIDTH:o_f].T.reshape(N_HEADS, HEAD_DIM, D)
        wvt = jnp.pad(wvt, ((0, 0), (0, VT_ROWS - HEAD_DIM), (0, 0))).reshape(N_HEADS * VT_ROWS, D).astype(BF16)
        wf = jnp.pad(w_in[l, :, o_f:o_x], ((0, 0), (0, LANES - N_HEADS))).astype(BF16)
        bf = jnp.pad(b_forget[l], (0, LANES - N_HEADS)).reshape(1, LANES).astype(F32)
        wxg = w_in[l, :, o_x:].astype(BF16)
        wri = jnp.concatenate([_block_diag(w_rec_gate[l]), _block_diag(w_in_gate[l])], axis=1).astype(BF16)
        bri = jnp.concatenate([b_rec_gate[l], b_in_gate[l]]).reshape(1, -1).astype(F32)
        qq, kk, vt, rec = _in_proj(x, wq, wk, wvt, vt_one, wf, bf, wxg, conv_w[l].astype(F32), row(conv_b[l]),
                                   wri, bri, row(lru_lambda[l]), blk_attn)

        attn = _attention(qq, kk, vt, blk_attn)

        x1, x1b, comb, pos, seg = _mix_out(
            x.reshape(T, D), attn.reshape(T, ATT_WIDTH), rec.reshape(T, LRU_WIDTH), row(mix_norm_g[l]),
            w_out[l].astype(BF16), row(ln1_g[l]), row(ln1_b[l]), wr, br, tri, alpha, tm_moe)

        seg = seg.reshape(-1, SUBLANES, LANES)[:, :N_GROUPS, :2].astype(jnp.int32).reshape(-1)
        wgu = jnp.concatenate([w_exp_gate[l], w_exp_up[l]], axis=-1).astype(BF16)
        x = _moe(seg, x1, x1b, comb, pos, wgu, w_exp_down[l].astype(BF16), row(ln2_g[l]), row(ln2_b[l]),
                 alpha, tm_moe).reshape(B, S, D).astype(x.dtype)
    return x
```
